```python
import jax
import jax.numpy as jnp
from jax import lax
import numpy as np


D_MODEL = 1024
BATCH = 16
SEQ = 2048
DEPTH = 4

GRID_W = 64
CTX_LEN = 256
HEAD_DIM = 64
MIX_WIDTH = D_MODEL
NA_HEADS = 8
NA_WIN_R = 8
NA_WIN_C = 16
NA_QCOLS = 16
NA_KCOLS = NA_QCOLS + NA_WIN_C
MLA_HEADS = 8
MLA_Q_RANK = 384
MLA_KV_RANK = 256
MLA_NOPE = 64
MLA_ROPE = 32
MLA_V = 64
ATTN_BLOCK = 128
SWA_Q_HEADS = 16
SWA_KV_HEADS = 2
SWA_WINDOW = 128
SWA_BLOCK = 128
D_FF = 2816
CONV_W = 3
ROPE_BASE = 10000.0
LN_EPS = 1e-6
RMS_EPS = 1e-6
NEG = -1e30
DEEPNORM_ALPHA = (2 * DEPTH) ** 0.25
DEEPNORM_BETA = (8 * DEPTH) ** -0.25
N_EVEN = (DEPTH + 1) // 2
N_ODD = DEPTH // 2
NA_WIDTH = NA_HEADS * HEAD_DIM
EVEN_SPLITS = [NA_WIDTH, 2 * NA_WIDTH, 3 * NA_WIDTH, 3 * NA_WIDTH + MLA_Q_RANK, 3 * NA_WIDTH + MLA_Q_RANK + MLA_KV_RANK]
EVEN_IN = 3 * NA_WIDTH + MLA_Q_RANK + MLA_KV_RANK + MLA_ROPE
ODD_SPLITS = [SWA_Q_HEADS * HEAD_DIM, (SWA_Q_HEADS + SWA_KV_HEADS) * HEAD_DIM]
ODD_IN = (SWA_Q_HEADS + 2 * SWA_KV_HEADS) * HEAD_DIM

kernel_name = 'hybrid_natten_mla_swa_dit_prefix'


def layer_norm(x):
    xf = x.astype(jnp.float32)
    mu = jnp.mean(xf, axis=-1, keepdims=True)
    var = jnp.mean(jnp.square(xf - mu), axis=-1, keepdims=True)
    return ((xf - mu) * lax.rsqrt(var + LN_EPS)).astype(x.dtype)


def rms_norm(x, g):
    xf = x.astype(jnp.float32)
    y = xf * lax.rsqrt(jnp.mean(jnp.square(xf), axis=-1, keepdims=True) + RMS_EPS)
    return y.astype(x.dtype) * g


def modulate(h, shift, scale):
    return h * (1 + scale) + shift


def axial_rope(n_tokens, rot_dim, dtype):
    axis_dim = rot_dim // 2
    t = jnp.arange(n_tokens)
    row = (t // GRID_W).astype(jnp.float32)[:, None]
    col = (t % GRID_W).astype(jnp.float32)[:, None]
    inv_freq = ROPE_BASE ** (-jnp.arange(0, axis_dim, 2, dtype=jnp.float32) / axis_dim)
    ar, ac = row * inv_freq, col * inv_freq
    ang = jnp.concatenate([ar, ar, ac, ac], axis=-1)
    return jnp.cos(ang).astype(dtype), jnp.sin(ang).astype(dtype)


def _rotate_half(v):
    a, b = jnp.split(v, 2, axis=-1)
    return jnp.concatenate([-b, a], axis=-1)


def apply_rope(x, cos, sin):
    xr, xc = jnp.split(x, 2, axis=-1)
    return x * cos + jnp.concatenate([_rotate_half(xr), _rotate_half(xc)], axis=-1) * sin


def dense_attention(q, k, v):
    B, T, H, dh = q.shape
    s = jnp.einsum('bqhd,bkhd->bhqk', q, k).astype(jnp.float32) * dh ** -0.5
    p = jax.nn.softmax(s, axis=-1).astype(v.dtype)
    return jnp.einsum('bhqk,bkhd->bqhd', p, v).reshape(B, T, H * dh)


def _na_column_tables():
    n_cb = GRID_W // NA_QCOLS
    q_col = np.arange(GRID_W).reshape(n_cb, NA_QCOLS)
    k_start = np.clip(np.arange(n_cb) * NA_QCOLS - NA_WIN_C // 2, 0, GRID_W - NA_KCOLS)
    k_col = k_start[:, None] + np.arange(NA_KCOLS)
    w_start = np.clip(q_col - NA_WIN_C // 2, 0, GRID_W - NA_WIN_C)
    kc = k_col[:, None, :]
    col_ok = (kc >= w_start[..., None]) & (kc < w_start[..., None] + NA_WIN_C)
    dcol_idx = np.clip(kc - q_col[..., None] + NA_WIN_C - 1, 0, 2 * NA_WIN_C - 2)
    return k_col, col_ok, dcol_idx


def neighbourhood_attention(q, k, v, k_ctx, v_ctx, rpb):
    B, S, H, dh = q.shape
    rows = S // GRID_W
    kr = min(NA_WIN_R, rows)
    n_cb = GRID_W // NA_QCOLS
    n_loc = kr * NA_KCOLS
    k_col, col_ok, dcol_idx = _na_column_tables()
    scale = dh ** -0.5
    qg = q.reshape(B, rows, n_cb, NA_QCOLS, H, dh)
    kg = k.reshape(B, rows, GRID_W, H, dh)
    vg = v.reshape(B, rows, GRID_W, H, dh)
    rpb_col = rpb[:, :, dcol_idx]
    mask = col_ok[:, :, None, :]

    def one_row(r):
        r0 = jnp.clip(r - NA_WIN_R // 2, 0, rows - kr)
        q_r = lax.dynamic_index_in_dim(qg, r, axis=1, keepdims=False)
        k_blk = lax.dynamic_slice_in_dim(kg, r0, kr, axis=1)[:, :, k_col]
        v_blk = lax.dynamic_slice_in_dim(vg, r0, kr, axis=1)[:, :, k_col]
        drow_idx = r0 + jnp.arange(kr) - r + NA_WIN_R - 1
        bias = jnp.take(rpb_col, drow_idx, axis=1).transpose(0, 2, 3, 1, 4)
        s_loc = jnp.einsum('bnqhd,bmnkhd->bhnqmk', q_r, k_blk) * scale + bias
        s_loc = jnp.where(mask, s_loc.astype(jnp.float32), NEG).reshape(B, H, n_cb, NA_QCOLS, n_loc)
        s_ctx = jnp.einsum('bnqhd,bchd->bhnqc', q_r, k_ctx).astype(jnp.float32) * scale
        p = jax.nn.softmax(jnp.concatenate([s_loc, s_ctx], axis=-1), axis=-1).astype(v.dtype)
        p_loc = p[..., :n_loc].reshape(B, H, n_cb, NA_QCOLS, kr, NA_KCOLS)
        o = (jnp.einsum('bhnqmk,bmnkhd->bnqhd', p_loc, v_blk)
             + jnp.einsum('bhnqc,bchd->bnqhd', p[..., n_loc:], v_ctx))
        return o.reshape(B, GRID_W, H * dh)

    out = lax.map(one_row, jnp.arange(rows))
    return out.swapaxes(0, 1).reshape(B, S, H * dh)


def mla_attention(qn, qr, kn, kr, v):
    B, T = qn.shape[:2]
    s = jnp.einsum('bqhd,bkhd->bhqk', qn, kn) + jnp.einsum('bqhd,bkd->bhqk', qr, kr)
    p = jax.nn.softmax(s.astype(jnp.float32) * (MLA_NOPE + MLA_ROPE) ** -0.5, axis=-1).astype(v.dtype)
    return jnp.einsum('bhqk,bkhd->bqhd', p, v).reshape(B, T, -1)


def mla_latent(qn, qr, kn, kr, v):
    B, S = qn.shape[:2]
    nb = S // ATTN_BLOCK
    blk = lambda t: t.reshape(B, nb, ATTN_BLOCK, *t.shape[2:]).swapaxes(0, 1)
    out = lax.map(lambda a: mla_attention(a[0], a[1], kn, kr, v), (blk(qn), blk(qr)))
    return out.swapaxes(0, 1).reshape(B, S, -1)


def window_gqa_latent(q, k, v, k_ctx, v_ctx, sink):
    B, S, Hq, dh = q.shape
    Hkv = k.shape[2]
    G = Hq // Hkv
    nb = S // SWA_BLOCK
    span = SWA_BLOCK + 2 * SWA_WINDOW
    scale = dh ** -0.5
    qb = q.reshape(B, nb, SWA_BLOCK, Hkv, G, dh).swapaxes(0, 1)
    pad = ((0, 0), (SWA_WINDOW, SWA_WINDOW), (0, 0), (0, 0))
    kp, vp = jnp.pad(k, pad), jnp.pad(v, pad)
    qi = jnp.arange(SWA_BLOCK)[:, None]
    kk = jnp.arange(span)[None, :]
    band = (kk >= qi) & (kk <= qi + 2 * SWA_WINDOW)
    s_sink = jnp.broadcast_to(sink.astype(jnp.float32).reshape(1, Hkv, G, 1, 1), (B, Hkv, G, SWA_BLOCK, 1))

    def one_block(args):
        n, q_n = args
        start = n * SWA_BLOCK
        k_n = lax.dynamic_slice_in_dim(kp, start, span, axis=1)
        v_n = lax.dynamic_slice_in_dim(vp, start, span, axis=1)
        key_pos = start - SWA_WINDOW + kk
        ok = band & (key_pos >= 0) & (key_pos < S)
        s_loc = jnp.einsum('bqhgd,bkhd->bhgqk', q_n, k_n).astype(jnp.float32) * scale
        s_loc = jnp.where(ok, s_loc, NEG)
        s_ctx = jnp.einsum('bqhgd,bchd->bhgqc', q_n, k_ctx).astype(jnp.float32) * scale
        p = jax.nn.softmax(jnp.concatenate([s_loc, s_ctx, s_sink], axis=-1), axis=-1).astype(v.dtype)
        o = (jnp.einsum('bhgqk,bkhd->bqhgd', p[..., :span], v_n)
             + jnp.einsum('bhgqc,bchd->bqhgd', p[..., span:-1], v_ctx))
        return o.reshape(B, SWA_BLOCK, Hq * dh)

    out = lax.map(one_block, (jnp.arange(nb), qb))
    return out.swapaxes(0, 1).reshape(B, S, Hq * dh)


def gqa_sink_dense(q, k, v, sink):
    B, T, Hq, dh = q.shape
    Hkv = k.shape[2]
    G = Hq // Hkv
    qg = q.reshape(B, T, Hkv, G, dh)
    s = jnp.einsum('bqhgd,bkhd->bhgqk', qg, k).astype(jnp.float32) * dh ** -0.5
    s_sink = jnp.broadcast_to(sink.astype(jnp.float32).reshape(1, Hkv, G, 1, 1), (B, Hkv, G, T, 1))
    p = jax.nn.softmax(jnp.concatenate([s, s_sink], axis=-1), axis=-1)[..., :-1].astype(v.dtype)
    return jnp.einsum('bhgqk,bkhd->bqhgd', p, v).reshape(B, T, Hq * dh)


def conv_ffn(h, w_up, b_up, conv_w, conv_b, w_down, b_down):
    T = h.shape[1]
    u = h @ w_up + b_up
    r = CONV_W // 2
    up = jnp.pad(u, ((0, 0), (r, r), (0, 0)))
    u = sum(up[:, i:i + T] * conv_w[i] for i in range(CONV_W)) + conv_b
    a, g = jnp.split(u, 2, axis=-1)
    return (a * jax.nn.silu(g)) @ w_down + b_down


def even_project(z, q_norm_g, w_uq, kv_norm_g, w_ukv):
    B, T, _ = z.shape
    qa, ka, va, cq, ckv, k_rope = jnp.split(z, EVEN_SPLITS, axis=-1)
    heads = lambda t: t.reshape(B, T, NA_HEADS, HEAD_DIM)
    q = (rms_norm(cq, q_norm_g) @ w_uq).reshape(B, T, MLA_HEADS, MLA_NOPE + MLA_ROPE)
    kv = (rms_norm(ckv, kv_norm_g) @ w_ukv).reshape(B, T, MLA_HEADS, MLA_NOPE + MLA_V)
    return (heads(qa), heads(ka), heads(va), q[..., :MLA_NOPE], q[..., MLA_NOPE:],
            kv[..., :MLA_NOPE], k_rope, kv[..., MLA_NOPE:])


def even_mixer(h, h_ctx, w_in, rpb, q_norm_g, w_uq, kv_norm_g, w_ukv, w_out, rope, with_ctx_out):
    cos, sin = rope
    qa, ka, va, qn, qr, kn, kr, vm = even_project(h @ w_in, q_norm_g, w_uq, kv_norm_g, w_ukv)
    qa_c, ka_c, va_c, qn_c, qr_c, kn_c, kr_c, vm_c = even_project(h_ctx @ w_in, q_norm_g, w_uq, kv_norm_g, w_ukv)
    qr = apply_rope(qr, cos[:, None, :], sin[:, None, :])
    kr = apply_rope(kr, cos, sin)
    o_a = neighbourhood_attention(qa, ka, va, ka_c, va_c, rpb)
    o_b = mla_latent(qn, qr, jnp.concatenate([kn, kn_c], axis=1), jnp.concatenate([kr, kr_c], axis=1),
                     jnp.concatenate([vm, vm_c], axis=1))
    y = jnp.concatenate([o_a, o_b], axis=-1) @ w_out
    if not with_ctx_out:
        return y, None
    o_ac = dense_attention(qa_c, ka_c, va_c)
    o_bc = mla_attention(qn_c, qr_c, kn_c, kr_c, vm_c)
    return y, jnp.concatenate([o_ac, o_bc], axis=-1) @ w_out


def odd_project(z):
    B, T, _ = z.shape
    q, k, v = jnp.split(z, ODD_SPLITS, axis=-1)
    return (q.reshape(B, T, SWA_Q_HEADS, HEAD_DIM), k.reshape(B, T, SWA_KV_HEADS, HEAD_DIM),
            v.reshape(B, T, SWA_KV_HEADS, HEAD_DIM))


def odd_mixer(h, h_ctx, w_in, sink, w_out, rope, with_ctx_out):
    cos, sin = rope
    q, k, v = odd_project(h @ w_in)
    q_c, k_c, v_c = odd_project(h_ctx @ w_in)
    q = apply_rope(q, cos[:, None, :], sin[:, None, :])
    k = apply_rope(k, cos[:, None, :], sin[:, None, :])
    y = window_gqa_latent(q, k, v, k_c, v_c, sink) @ w_out
    if not with_ctx_out:
        return y, None
    return y, gqa_sink_dense(q_c, k_c, v_c, sink) @ w_out


def setup_inputs(seed: int = 0) -> dict:
    key = jax.random.key(seed)
    keys = iter(jax.random.split(key, 24))

    def nrm(shape, scale):
        return jax.random.normal(next(keys), shape, jnp.float32) * scale

    D = D_MODEL
    F2 = 2 * D_FF
    return {
        'x': nrm((BATCH, SEQ, D), 1.0),
        'c': nrm((BATCH, D), 1.0),
        'ctx': nrm((BATCH, CTX_LEN, D), 1.0),
        'c_ctx': nrm((D,), 1.0),
        'w_ada': nrm((DEPTH, D, 6 * D), 0.5 * D ** -0.5),
        'b_ada': nrm((DEPTH, 6 * D), 0.01),
        'na_rpb': nrm((N_EVEN, NA_HEADS, 2 * NA_WIN_R - 1, 2 * NA_WIN_C - 1), 0.1),
        'w_in_even': nrm((N_EVEN, D, EVEN_IN), D ** -0.5),
        'mla_q_norm': 1.0 + nrm((N_EVEN, MLA_Q_RANK), 0.01),
        'w_uq': nrm((N_EVEN, MLA_Q_RANK, MLA_HEADS * (MLA_NOPE + MLA_ROPE)), MLA_Q_RANK ** -0.5),
        'mla_kv_norm': 1.0 + nrm((N_EVEN, MLA_KV_RANK), 0.01),
        'w_ukv': nrm((N_EVEN, MLA_KV_RANK, MLA_HEADS * (MLA_NOPE + MLA_V)), MLA_KV_RANK ** -0.5),
        'w_out_even': nrm((N_EVEN, MIX_WIDTH, D), DEEPNORM_BETA * MIX_WIDTH ** -0.5),
        'w_in_odd': nrm((N_ODD, D, ODD_IN), D ** -0.5),
        'sinks': nrm((N_ODD, SWA_Q_HEADS), 0.5),
        'w_out_odd': nrm((N_ODD, SWA_Q_HEADS * HEAD_DIM, D), DEEPNORM_BETA * (SWA_Q_HEADS * HEAD_DIM) ** -0.5),
        'w_up': nrm((DEPTH, D, F2), D ** -0.5),
        'b_up': nrm((DEPTH, F2), 0.01),
        'conv_w': nrm((DEPTH, CONV_W, F2), CONV_W ** -0.5),
        'conv_b': nrm((DEPTH, F2), 0.01),
        'w_down': nrm((DEPTH, D_FF, D), DEEPNORM_BETA * D_FF ** -0.5),
        'b_down': nrm((DEPTH, D), 0.01),
    }


def reference(x, c, ctx, c_ctx, w_ada, b_ada, na_rpb, w_in_even, mla_q_norm, w_uq, mla_kv_norm, w_ukv,
              w_out_even, w_in_odd, sinks, w_out_odd, w_up, b_up, conv_w, conv_b, w_down, b_down):
    S = x.shape[1]
    rope_mla = axial_rope(S, MLA_ROPE, x.dtype)
    rope_swa = axial_rope(S, HEAD_DIM, x.dtype)
    mod_lat = jnp.einsum('bd,ldk->lbk', jax.nn.silu(c), w_ada) + b_ada[:, None, :]
    mod_ctx = jnp.einsum('d,ldk->lk', jax.nn.silu(c_ctx), w_ada) + b_ada
    z = ctx
    for l in range(DEPTH):
        i = l // 2
        with_ctx = l < DEPTH - 1
        sh_m, sc_m, g_m, sh_f, sc_f, g_f = jnp.split(mod_lat[l][:, None, :], 6, axis=-1)
        csh_m, csc_m, cg_m, csh_f, csc_f, cg_f = jnp.split(mod_ctx[l], 6, axis=-1)
        h, hc = modulate(x, sh_m, sc_m), modulate(z, csh_m, csc_m)
        if l % 2 == 0:
            y, y_c = even_mixer(h, hc, w_in_even[i], na_rpb[i], mla_q_norm[i], w_uq[i], mla_kv_norm[i],
                                w_ukv[i], w_out_even[i], rope_mla, with_ctx)
        else:
            y, y_c = odd_mixer(h, hc, w_in_odd[i], sinks[i], w_out_odd[i], rope_swa, with_ctx)
        ffn = (w_up[l], b_up[l], conv_w[l], conv_b[l], w_down[l], b_down[l])
        x = layer_norm(DEEPNORM_ALPHA * x + g_m * y)
        x = layer_norm(DEEPNORM_ALPHA * x + g_f * conv_ffn(modulate(x, sh_f, sc_f), *ffn))
        if with_ctx:
            z = layer_norm(DEEPNORM_ALPHA * z + cg_m * y_c)
            z = layer_norm(DEEPNORM_ALPHA * z + cg_f * conv_ffn(modulate(z, csh_f, csc_f), *ffn))
    return x
```

```python
import functools

import numpy as np
import jax
import jax.numpy as jnp
from jax import lax
from jax.experimental import pallas as pl
from jax.experimental.pallas import tpu as pltpu

D_MODEL = 1024
BATCH = 16
SEQ = 2048
CTX_LEN = 256
T_ALL = SEQ + CTX_LEN
DEPTH = 4
GRID_W = 64
HEAD_DIM = 64
NA_HEADS = 8
NA_WIN_R = 8
NA_WIN_C = 16
MLA_HEADS = 8
MLA_Q_RANK = 384
MLA_KV_RANK = 256
MLA_NOPE = 64
MLA_ROPE = 32
MLA_V = 64
SWA_Q_HEADS = 16
SWA_KV_HEADS = 2
SWA_WINDOW = 128
SWA_BLOCK = 128
D_FF = 2816
CONV_W = 3
ROPE_BASE = 10000.0
LN_EPS = 1e-6
RMS_EPS = 1e-6
NEG = -1e30
DEEPNORM_ALPHA = (2 * DEPTH) ** 0.25
NA_WIDTH = NA_HEADS * HEAD_DIM

LANES = 128
TM = 256
NT_LAT = SEQ // TM
NT_ALL = T_ALL // TM
HALO = 8
FF_CHUNK = 256
N_FF_CHUNKS = D_FF // FF_CHUNK
NA_QROWS = TM // GRID_W
NA_KROWS = 3 * NA_QROWS
VMEM_LIMIT = 56 * 1024 * 1024

F32 = jnp.float32
BF16 = jnp.bfloat16


def _params(sem):
    return pltpu.CompilerParams(dimension_semantics=sem, vmem_limit_bytes=VMEM_LIMIT)


def _dot(a, b):
    return jnp.dot(a, b, preferred_element_type=F32)


def _dot_nt(a, b):
    return lax.dot_general(a, b, (((1,), (1,)), ((), ())), preferred_element_type=F32)


def _layer_norm(x):
    mu = jnp.mean(x, axis=-1, keepdims=True)
    xc = x - mu
    var = jnp.mean(xc * xc, axis=-1, keepdims=True)
    return xc * lax.rsqrt(var + LN_EPS)


def _rms_norm(x, g):
    return x * lax.rsqrt(jnp.mean(x * x, axis=-1, keepdims=True) + RMS_EPS) * g


def _silu(x):
    return x / (1.0 + jnp.exp(-x))


def _rope(x, cos, sin, chunk):
    lane = lax.broadcasted_iota(jnp.int32, x.shape, 1)
    first = (lane % (2 * chunk)) < chunk
    rot = jnp.where(first, pltpu.roll(x, LANES - chunk, 1), pltpu.roll(x, chunk, 1))
    return x * cos + rot * sin


def _attend(s_list, v_list, sink=None):
    m = functools.reduce(jnp.maximum, [jnp.max(s, axis=-1, keepdims=True) for s in s_list])
    if sink is not None:
        m = jnp.maximum(m, sink)
    l = None
    o = None
    for s, v in zip(s_list, v_list):
        p = jnp.exp(s - m)
        ps = jnp.sum(p, axis=-1, keepdims=True)
        po = _dot(p.astype(BF16), v)
        l = ps if l is None else l + ps
        o = po if o is None else o + po
    if sink is not None:
        l = l + jnp.exp(sink - m)
    return o / l


def _low_half(shape):
    return lax.broadcasted_iota(jnp.int32, shape, 1) < HEAD_DIM


def _mod_kernel(c_ref, w_ref, b_ref, o_ref):
    a = _silu(c_ref[...]).astype(BF16)
    o_ref[...] = _dot(a, w_ref[...].astype(BF16)) + b_ref[...]


def _modulation(cs, w_ada, b_ada):
    rows = cs.shape[0]
    tn = 1536
    return pl.pallas_call(
        _mod_kernel,
        grid=(DEPTH, 6 * D_MODEL // tn),
        in_specs=[
            pl.BlockSpec((rows, D_MODEL), lambda l, n: (0, 0)),
            pl.BlockSpec((None, D_MODEL, tn), lambda l, n: (l, 0, n)),
            pl.BlockSpec((None, 1, tn), lambda l, n: (l, 0, n)),
        ],
        out_specs=pl.BlockSpec((None, rows, tn), lambda l, n: (l, 0, n)),
        out_shape=jax.ShapeDtypeStruct((DEPTH, rows, 6 * D_MODEL), F32),
        compiler_params=_params(("arbitrary", "arbitrary")),
        name="adaln_mod",
    )(cs, w_ada, b_ada.reshape(DEPTH, 1, 6 * D_MODEL))


def _mod_spec():
    return pl.BlockSpec((None, 1, D_MODEL), lambda b, t: (2 * b + jnp.where(t >= NT_LAT, 1, 0), 0, 0))


def _const_spec(shape):
    return pl.BlockSpec(shape, lambda *_: (0,) * len(shape))


def _tok_spec(width):
    return pl.BlockSpec((None, TM, width), lambda b, t: (b, t, 0))


def _inproj_even_kernel(x_ref, sh_ref, sc_ref, w_in_ref, qg_ref, w_uq_ref, kg_ref, w_ukk_ref, w_ukv_ref,
                        cq_ref, sq_ref, ck_ref, sk_ref,
                        qa_ref, ka_ref, va_ref, qm_ref, km_ref, vm_ref):
    h = (x_ref[...] * (1.0 + sc_ref[...]) + sh_ref[...]).astype(BF16)
    z = _dot(h, w_in_ref[...])
    w = NA_WIDTH
    qa_ref[...] = (z[:, :w] * HEAD_DIM ** -0.5).astype(BF16)
    ka_ref[...] = z[:, w:2 * w].astype(BF16)
    va_ref[...] = z[:, 2 * w:3 * w].astype(BF16)
    o1 = 3 * w
    o2 = o1 + MLA_Q_RANK
    o3 = o2 + MLA_KV_RANK
    cq = _rms_norm(z[:, o1:o2], qg_ref[...]).astype(BF16)
    ckv = _rms_norm(z[:, o2:o3], kg_ref[...]).astype(BF16)
    q = _dot(cq, w_uq_ref[...])
    kn = _dot(ckv, w_ukk_ref[...])
    vm_ref[...] = _dot(ckv, w_ukv_ref[...]).astype(BF16)
    kr = _rope(z[:, o3:o3 + LANES], ck_ref[...], sk_ref[...], MLA_ROPE // 4)
    cq_t = cq_ref[...]
    sq_t = sq_ref[...]
    for hh in range(MLA_HEADS):
        sl = slice(hh * LANES, (hh + 1) * LANES)
        qm_ref[:, sl] = _rope(q[:, sl], cq_t, sq_t, MLA_ROPE // 4).astype(BF16)
        km_ref[:, sl] = (kn[:, sl] + kr).astype(BF16)


def _inproj_even(x, sh, sc, w_in, qg, w_uq, kg, w_ukk, w_ukv, tabs):
    cq, sq, ck, sk = tabs
    tab = pl.BlockSpec((TM, LANES), lambda b, t: (t, 0))
    n_in = w_in.shape[1]
    outs = [(NA_WIDTH, BF16)] * 3 + [(MLA_HEADS * LANES, BF16)] * 2 + [(MLA_HEADS * MLA_V, BF16)]
    return pl.pallas_call(
        _inproj_even_kernel,
        grid=(BATCH, NT_ALL),
        in_specs=[_tok_spec(D_MODEL), _mod_spec(), _mod_spec(),
                  _const_spec((D_MODEL, n_in)),
                  _const_spec((1, MLA_Q_RANK)), _const_spec(w_uq.shape),
                  _const_spec((1, MLA_KV_RANK)), _const_spec(w_ukk.shape), _const_spec(w_ukv.shape),
                  tab, tab, tab, tab],
        out_specs=[_tok_spec(wd) for wd, _ in outs],
        out_shape=[jax.ShapeDtypeStruct((BATCH, T_ALL, wd), dt) for wd, dt in outs],
        compiler_params=_params(("arbitrary", "arbitrary")),
        name="inproj_even",
    )(x, sh, sc, w_in, qg, w_uq, kg, w_ukk, w_ukv, cq, sq, ck, sk)


def _na_kernel(q_ref, k0_ref, k1_ref, k2_ref, kc_ref, v0_ref, v1_ref, v2_ref, vc_ref, bias_ref, o_ref):
    j = pl.program_id(0)
    q = q_ref[...]
    lo = _low_half(q.shape)
    zero = jnp.zeros_like(q)

    def both_heads(fn):
        o0 = fn(0, jnp.where(lo, q, zero))
        o1 = fn(1, jnp.where(lo, zero, q))
        o_ref[...] = jnp.where(lo, o0, o1).astype(o_ref.dtype)

    @pl.when(j < NT_LAT)
    def _():
        def fn(e, qe):
            s = [_dot_nt(qe, kr[...]) + bias_ref[e, :, i * TM:(i + 1) * TM]
                 for i, kr in enumerate((k0_ref, k1_ref, k2_ref))]
            s.append(_dot_nt(qe, kc_ref[...]))
            return _attend(s, [v0_ref[...], v1_ref[...], v2_ref[...], vc_ref[...]])
        both_heads(fn)

    @pl.when(j >= NT_LAT)
    def _():
        both_heads(lambda e, qe: _attend([_dot_nt(qe, kc_ref[...])], [vc_ref[...]]))


def _na_attention(qa, ka, va, bias):
    def kstart(j):
        return jnp.clip(j - 1, 0, NT_LAT - 3)

    def bias_class(j):
        return jnp.where(j == 0, 0, jnp.where(j == NT_LAT - 1, 2, 1))

    blk = (None, TM, LANES)
    q_spec = pl.BlockSpec(blk, lambda j, hp, b: (b, j, hp))
    kv_specs = [pl.BlockSpec(blk, functools.partial(lambda i, j, hp, b: (b, kstart(j) + i, hp), i))
                for i in range(3)]
    ctx_spec = pl.BlockSpec(blk, lambda j, hp, b: (b, NT_LAT, hp))
    bias_spec = pl.BlockSpec((None, 2, TM, 3 * TM), lambda j, hp, b: (bias_class(j), hp, 0, 0))
    return pl.pallas_call(
        _na_kernel,
        grid=(NT_ALL, NA_HEADS // 2, BATCH),
        in_specs=[q_spec] + kv_specs + [ctx_spec] + kv_specs + [ctx_spec, bias_spec],
        out_specs=q_spec,
        out_shape=jax.ShapeDtypeStruct((BATCH, T_ALL, NA_WIDTH), BF16),
        compiler_params=_params(("arbitrary", "arbitrary", "arbitrary")),
        name="na_attention",
    )(qa, ka, ka, ka, ka, va, va, va, va, bias)


def _na_bias_indices():
    rows = SEQ // GRID_W
    kc = np.arange(GRID_W)
    qc = np.arange(GRID_W)
    w_start = np.clip(qc - NA_WIN_C // 2, 0, GRID_W - NA_WIN_C)
    col_ok = (kc[None, :] >= w_start[:, None]) & (kc[None, :] < w_start[:, None] + NA_WIN_C)
    dcol = np.clip(kc[None, :] - qc[:, None] + NA_WIN_C - 1, 0, 2 * NA_WIN_C - 2)
    drs, dcs, oks = [], [], []
    for j in (0, 1, NT_LAT - 1):
        r = NA_QROWS * j + np.arange(NA_QROWS)
        r0 = np.clip(r - NA_WIN_R // 2, 0, rows - NA_WIN_R)
        start = NA_QROWS * int(np.clip(j - 1, 0, NT_LAT - 3))
        krow = start + np.arange(NA_KROWS)
        row_ok = (krow[None, :] >= r0[:, None]) & (krow[None, :] < r0[:, None] + NA_WIN_R)
        drow = np.clip(krow[None, :] - r[:, None] + NA_WIN_R - 1, 0, 2 * NA_WIN_R - 2)
        shape = (NA_QROWS, GRID_W, NA_KROWS, GRID_W)
        ok = np.broadcast_to(row_ok[:, None, :, None] & col_ok[None, :, None, :], shape)
        dr = np.broadcast_to(drow[:, None, :, None], shape)
        dc = np.broadcast_to(dcol[None, :, None, :], shape)
        flat = (NA_QROWS * GRID_W, NA_KROWS * GRID_W)
        drs.append(dr.reshape(flat))
        dcs.append(dc.reshape(flat))
        oks.append(ok.reshape(flat))
    return np.stack(drs), np.stack(dcs), np.stack(oks)


def _na_bias(rpb):
    dr, dc, ok = _na_bias_indices()
    b = jnp.where(ok[None], rpb[:, dr, dc], NEG)
    return b.transpose(1, 0, 2, 3)


MLA_KCHUNK = 768


def _mla_kernel(q_ref, k_ref, v_ref, o_ref):
    jq = pl.program_id(2)
    lo = _low_half((TM, LANES))

    def run(k_lo, k_hi):
        outs = []
        for e in (0, 1):
            sl = slice(e * LANES, (e + 1) * LANES)
            qe = q_ref[:, sl]
            s, v = [], []
            for c0 in range(k_lo, k_hi, MLA_KCHUNK):
                c1 = min(c0 + MLA_KCHUNK, k_hi)
                s.append(_dot_nt(qe, k_ref[c0:c1, sl]))
                v.append(v_ref[c0:c1, :])
            outs.append(_attend(s, v))
        o_ref[...] = jnp.where(lo, outs[0], outs[1]).astype(o_ref.dtype)

    pl.when(jq < NT_LAT)(lambda: run(0, T_ALL))
    pl.when(jq >= NT_LAT)(lambda: run(SEQ, T_ALL))


def _mla_attention(qm, km, vm):
    return pl.pallas_call(
        _mla_kernel,
        grid=(BATCH, MLA_HEADS // 2, NT_ALL),
        in_specs=[pl.BlockSpec((None, TM, 2 * LANES), lambda b, hp, j: (b, j, hp)),
                  pl.BlockSpec((None, T_ALL, 2 * LANES), lambda b, hp, j: (b, 0, hp)),
                  pl.BlockSpec((None, T_ALL, LANES), lambda b, hp, j: (b, 0, hp))],
        out_specs=pl.BlockSpec((None, TM, LANES), lambda b, hp, j: (b, j, hp)),
        out_shape=jax.ShapeDtypeStruct((BATCH, T_ALL, MLA_HEADS * MLA_V), BF16),
        compiler_params=_params(("arbitrary", "arbitrary", "arbitrary")),
        name="mla_attention",
    )(qm, km, vm)


def _inproj_odd_kernel(x_ref, sh_ref, sc_ref, w_in_ref, cq_ref, sq_ref, ck_ref, sk_ref, q_ref, k_ref, v_ref):
    h = (x_ref[...] * (1.0 + sc_ref[...]) + sh_ref[...]).astype(BF16)
    z = _dot(h, w_in_ref[...])
    nq = SWA_Q_HEADS * HEAD_DIM
    nk = 2 * SWA_KV_HEADS * HEAD_DIM
    cq_t, sq_t, ck_t, sk_t = cq_ref[...], sq_ref[...], ck_ref[...], sk_ref[...]
    for g in range(nq // LANES):
        sl = slice(g * LANES, (g + 1) * LANES)
        q_ref[:, sl] = _rope(z[:, sl], cq_t, sq_t, HEAD_DIM // 4).astype(BF16)
    for g in range(nk // LANES):
        sl = slice(g * LANES, (g + 1) * LANES)
        k_ref[:, sl] = _rope(z[:, nq + g * LANES:nq + (g + 1) * LANES], ck_t, sk_t, HEAD_DIM // 4).astype(BF16)
    v_ref[...] = z[:, nq + nk:].astype(BF16)


def _inproj_odd(x, sh, sc, w_in, tabs):
    cq, sq, ck, sk = tabs
    tab = pl.BlockSpec((TM, LANES), lambda b, t: (t, 0))
    nq = SWA_Q_HEADS * HEAD_DIM
    nk = 2 * SWA_KV_HEADS * HEAD_DIM
    outs = [nq, nk, nk]
    return pl.pallas_call(
        _inproj_odd_kernel,
        grid=(BATCH, NT_ALL),
        in_specs=[_tok_spec(D_MODEL), _mod_spec(), _mod_spec(), _const_spec(w_in.shape), tab, tab, tab, tab],
        out_specs=[_tok_spec(wd) for wd in outs],
        out_shape=[jax.ShapeDtypeStruct((BATCH, T_ALL, wd), BF16) for wd in outs],
        compiler_params=_params(("arbitrary", "arbitrary")),
        name="inproj_odd",
    )(x, sh, sc, w_in, cq, sq, ck, sk)


SWA_GROUP = SWA_Q_HEADS // SWA_KV_HEADS
N_QBLK_LAT = SEQ // SWA_BLOCK


def _swa_kernel(q_ref, k0_ref, k1_ref, k2_ref, kc_ref, v0_ref, v1_ref, v2_ref, vc_ref, sink_ref, o_ref):
    n = pl.program_id(2)
    q = q_ref[...]
    lo = _low_half((SWA_BLOCK, LANES))
    parts = []
    for pr in range(SWA_GROUP // 2):
        qp = q[:, pr * LANES:(pr + 1) * LANES]
        zero = jnp.zeros_like(qp)
        parts.append(jnp.where(lo, qp, zero))
        parts.append(jnp.where(lo, zero, qp))
    qs = jnp.concatenate(parts, axis=0)
    kcat = jnp.concatenate([k0_ref[...], k1_ref[...], k2_ref[...]], axis=0)
    vcat = jnp.concatenate([v0_ref[...], v1_ref[...], v2_ref[...]], axis=0)
    span = 3 * SWA_BLOCK
    rows = SWA_GROUP * SWA_BLOCK
    qi = lax.broadcasted_iota(jnp.int32, (rows, span), 0) % SWA_BLOCK
    kk = lax.broadcasted_iota(jnp.int32, (rows, span), 1)
    key_pos = (n - 1) * SWA_BLOCK + kk
    rel = kk - SWA_BLOCK - qi
    ok = (rel >= -SWA_WINDOW) & (rel <= SWA_WINDOW) & (key_pos >= 0) & (key_pos < SEQ) & (n < N_QBLK_LAT)
    s_loc = jnp.where(ok, _dot_nt(qs, kcat), NEG)
    s_ctx = _dot_nt(qs, kc_ref[...])
    o = _attend([s_loc, s_ctx], [vcat, vc_ref[...]], sink=sink_ref[...])
    for pr in range(SWA_GROUP // 2):
        oa = o[(2 * pr) * SWA_BLOCK:(2 * pr + 1) * SWA_BLOCK]
        ob = o[(2 * pr + 1) * SWA_BLOCK:(2 * pr + 2) * SWA_BLOCK]
        o_ref[:, pr * LANES:(pr + 1) * LANES] = jnp.where(lo, oa, ob).astype(o_ref.dtype)


def _swa_attention(q, kd, vd, sink_col, n_qblk):
    gw = SWA_GROUP * HEAD_DIM
    q_spec = pl.BlockSpec((None, SWA_BLOCK, gw), lambda b, g, n: (b, n, g))
    kv_specs = [pl.BlockSpec((None, SWA_BLOCK, LANES),
                             functools.partial(lambda i, b, g, n: (b, jnp.clip(n - 1 + i, 0, N_QBLK_LAT - 1), g), i))
                for i in range(3)]
    ctx_spec = pl.BlockSpec((None, CTX_LEN, LANES), lambda b, g, n: (b, SEQ // CTX_LEN, g))
    sink_spec = pl.BlockSpec((None, SWA_GROUP * SWA_BLOCK, 1), lambda b, g, n: (g, 0, 0))
    return pl.pallas_call(
        _swa_kernel,
        grid=(BATCH, SWA_KV_HEADS, n_qblk),
        in_specs=[q_spec] + kv_specs + [ctx_spec] + kv_specs + [ctx_spec, sink_spec],
        out_specs=q_spec,
        out_shape=jax.ShapeDtypeStruct((BATCH, n_qblk * SWA_BLOCK, SWA_Q_HEADS * HEAD_DIM), BF16),
        compiler_params=_params(("arbitrary", "arbitrary", "arbitrary")),
        name="swa_attention",
    )(q, kd, kd, kd, kd, vd, vd, vd, vd, sink_col)


def _outproj_kernel(n_o, *refs):
    o_refs, w_refs = refs[:n_o], refs[n_o:2 * n_o]
    x_ref, g_ref, out_ref = refs[2 * n_o:]
    y = None
    for o_r, w_r in zip(o_refs, w_refs):
        d = _dot(o_r[...], w_r[...])
        y = d if y is None else y + d
    out_ref[...] = _layer_norm(DEEPNORM_ALPHA * x_ref[...] + g_ref[...] * y)


def _outproj(o_list, w_list, x, gate, n_tiles):
    return pl.pallas_call(
        functools.partial(_outproj_kernel, len(o_list)),
        grid=(BATCH, n_tiles),
        in_specs=[_tok_spec(o.shape[-1]) for o in o_list] + [_const_spec(w.shape) for w in w_list]
                 + [_tok_spec(D_MODEL), _mod_spec()],
        out_specs=_tok_spec(D_MODEL),
        out_shape=jax.ShapeDtypeStruct((BATCH, n_tiles * TM, D_MODEL), F32),
        compiler_params=_params(("arbitrary", "arbitrary")),
        name="outproj_ln",
    )(*o_list, *w_list, x, gate)


def _ffn_kernel(n_tiles, x_ref, xp_ref, xn_ref, sh_ref, sc_ref, g_ref, w_up_ref, b_up_ref, cw_ref, cb_ref,
                w_dn_ref, b_dn_ref, out_ref, acc_ref):
    t = pl.program_id(1)
    scale = 1.0 + sc_ref[...]
    shift = sh_ref[...]
    x = x_ref[...]
    rows = TM + 2 * HALO
    hcat = jnp.concatenate([xp_ref[...] * scale + shift, x * scale + shift, xn_ref[...] * scale + shift], axis=0)
    h = hcat.astype(BF16)
    prev_ok = (t != 0) & (t != NT_LAT)
    next_ok = (t != NT_LAT - 1) & (t != n_tiles - 1)
    ridx = lax.broadcasted_iota(jnp.int32, (rows, 1), 0)
    row_ok = ((ridx >= HALO) | prev_ok) & ((ridx < HALO + TM) | next_ok)
    for c in range(N_FF_CHUNKS):
        u = _dot(h, w_up_ref[c]) + b_up_ref[c]
        u = jnp.where(row_ok, u, 0.0)
        cw = cw_ref[c]
        v = (cw[0:1] * pltpu.roll(u, 1, 0) + cw[1:2] * u + cw[2:3] * pltpu.roll(u, rows - 1, 0)
             + cb_ref[c])[HALO:HALO + TM]
        act = (v[:, :FF_CHUNK] * _silu(v[:, FF_CHUNK:])).astype(BF16)
        d = _dot(act, w_dn_ref[c])
        if c == 0:
            acc_ref[...] = d
        else:
            acc_ref[...] += d
    y = acc_ref[...] + b_dn_ref[...]
    out_ref[...] = _layer_norm(DEEPNORM_ALPHA * x + g_ref[...] * y)


def _ffn(x, sh, sc, gate, w_up, b_up, cw, cb, w_dn, b_dn, n_tiles):
    per_tile = TM // HALO
    last = n_tiles * per_tile - 1
    prev_spec = pl.BlockSpec((None, HALO, D_MODEL), lambda b, t: (b, jnp.maximum(t * per_tile - 1, 0), 0))
    next_spec = pl.BlockSpec((None, HALO, D_MODEL), lambda b, t: (b, jnp.minimum((t + 1) * per_tile, last), 0))
    return pl.pallas_call(
        functools.partial(_ffn_kernel, n_tiles),
        grid=(BATCH, n_tiles),
        in_specs=[_tok_spec(D_MODEL), prev_spec, next_spec, _mod_spec(), _mod_spec(), _mod_spec(),
                  _const_spec(w_up.shape), _const_spec(b_up.shape), _const_spec(cw.shape), _const_spec(cb.shape),
                  _const_spec(w_dn.shape), _const_spec(b_dn.shape)],
        out_specs=_tok_spec(D_MODEL),
        out_shape=jax.ShapeDtypeStruct((BATCH, n_tiles * TM, D_MODEL), F32),
        scratch_shapes=[pltpu.VMEM((TM, D_MODEL), F32)],
        compiler_params=_params(("arbitrary", "arbitrary")),
        name="conv_ffn_ln",
    )(x, x, x, sh, sc, gate, w_up, b_up, cw, cb, w_dn, b_dn)


def _rope_tables(rot_dim):
    axis_dim = rot_dim // 2
    t = jnp.arange(SEQ)
    row = (t // GRID_W).astype(F32)[:, None]
    col = (t % GRID_W).astype(F32)[:, None]
    inv_freq = ROPE_BASE ** (-jnp.arange(0, axis_dim, 2, dtype=F32) / axis_dim)
    ar, ac = row * inv_freq, col * inv_freq
    ang = jnp.concatenate([ar, ar, ac, ac], axis=-1)
    sign = np.where(np.arange(rot_dim) % axis_dim < axis_dim // 2, -1.0, 1.0).astype(np.float32)
    return jnp.cos(ang), jnp.sin(ang) * sign


def _with_ctx_rows(cos, sin):
    ones = jnp.ones((CTX_LEN, cos.shape[1]), F32)
    return jnp.concatenate([cos, ones], axis=0), jnp.concatenate([sin, jnp.zeros_like(ones)], axis=0)


def _mla_tables():
    cos, sin = _rope_tables(MLA_ROPE)
    pad_l = jnp.ones((SEQ, MLA_NOPE), F32)
    pad_r = jnp.ones((SEQ, LANES - MLA_NOPE - MLA_ROPE), F32)
    cos = jnp.concatenate([pad_l, cos, pad_r], axis=-1)
    sin = jnp.concatenate([0 * pad_l, sin, 0 * pad_r], axis=-1)
    cos, sin = _with_ctx_rows(cos, sin)
    qs = (MLA_NOPE + MLA_ROPE) ** -0.5
    return cos * qs, sin * qs, cos, sin


def _swa_tables():
    cos, sin = _rope_tables(HEAD_DIM)
    cos, sin = _with_ctx_rows(jnp.tile(cos, (1, 2)), jnp.tile(sin, (1, 2)))
    qs = HEAD_DIM ** -0.5
    return cos * qs, sin * qs, cos, sin


def _even_weights(w_in, w_uq, w_ukv, w_out):
    o3 = 3 * NA_WIDTH + MLA_Q_RANK + MLA_KV_RANK
    pad_l = jnp.zeros((D_MODEL, MLA_NOPE), F32)
    pad_r = jnp.zeros((D_MODEL, LANES - MLA_NOPE - MLA_ROPE), F32)
    w_in_p = jnp.concatenate([w_in[:, :o3], pad_l, w_in[:, o3:], pad_r], axis=-1).astype(BF16)
    uq = w_uq.reshape(MLA_Q_RANK, MLA_HEADS, MLA_NOPE + MLA_ROPE)
    uq = jnp.pad(uq, ((0, 0), (0, 0), (0, LANES - MLA_NOPE - MLA_ROPE))).reshape(MLA_Q_RANK, MLA_HEADS * LANES)
    ukv = w_ukv.reshape(MLA_KV_RANK, MLA_HEADS, MLA_NOPE + MLA_V)
    ukk = jnp.pad(ukv[:, :, :MLA_NOPE], ((0, 0), (0, 0), (0, LANES - MLA_NOPE))).reshape(MLA_KV_RANK, MLA_HEADS * LANES)
    ukvv = ukv[:, :, MLA_NOPE:].reshape(MLA_KV_RANK, MLA_HEADS * MLA_V)
    wo = w_out.astype(BF16)
    return w_in_p, uq.astype(BF16), ukk.astype(BF16), ukvv.astype(BF16), wo[:NA_WIDTH], wo[NA_WIDTH:]


def _odd_weights(w_in, w_out):
    nq = SWA_Q_HEADS * HEAD_DIM
    nkv = SWA_KV_HEADS * HEAD_DIM
    dup = lambda w: jnp.repeat(w.reshape(D_MODEL, SWA_KV_HEADS, 1, HEAD_DIM), 2, axis=2).reshape(D_MODEL, 2 * nkv)
    w_in_p = jnp.concatenate([w_in[:, :nq], dup(w_in[:, nq:nq + nkv]), dup(w_in[:, nq + nkv:])], axis=-1)
    return w_in_p.astype(BF16), w_out.astype(BF16)


def _ffn_weights(w_up, b_up, conv_w, conv_b, w_down, b_down):
    def chunked(a):
        lead = a.shape[:-1]
        a = a.reshape(*lead, 2, N_FF_CHUNKS, FF_CHUNK)
        a = jnp.moveaxis(a, -2, 0)
        return a.reshape(N_FF_CHUNKS, *lead, 2 * FF_CHUNK)
    return (chunked(w_up).astype(BF16), chunked(b_up[None]), chunked(conv_w), chunked(conv_b[None]),
            w_down.reshape(N_FF_CHUNKS, FF_CHUNK, D_MODEL).astype(BF16), b_down[None])


def kernel(x, c, ctx, c_ctx, w_ada, b_ada, na_rpb, w_in_even, mla_q_norm, w_uq, mla_kv_norm, w_ukv, w_out_even,
           w_in_odd, sinks, w_out_odd, w_up, b_up, conv_w, conv_b, w_down, b_down):
    cs = jnp.concatenate([c, c_ctx[None], jnp.zeros((7, D_MODEL), F32)], axis=0)
    mod = _modulation(cs, w_ada, b_ada)
    mod_lat = mod[:, :BATCH].reshape(DEPTH, BATCH, 1, 6, D_MODEL)
    mod_ctx = jnp.broadcast_to(mod[:, BATCH].reshape(DEPTH, 1, 1, 6, D_MODEL), mod_lat.shape)
    mods = jnp.concatenate([mod_lat, mod_ctx], axis=2).reshape(DEPTH, 2 * BATCH, 6, 1, D_MODEL)

    mla_tabs = _mla_tables()
    swa_tabs = _swa_tables()
    xs = jnp.concatenate([x, ctx], axis=1)
    for l in range(DEPTH):
        i = l // 2
        last = l == DEPTH - 1
        n_tiles = NT_LAT if last else NT_ALL
        sh_m, sc_m, g_m, sh_f, sc_f, g_f = (mods[l, :, k] for k in range(6))
        if l % 2 == 0:
            w_in_p, uq, ukk, ukvv, wo_a, wo_b = _even_weights(w_in_even[i], w_uq[i], w_ukv[i], w_out_even[i])
            qa, ka, va, qm, km, vm = _inproj_even(xs, sh_m, sc_m, w_in_p, mla_q_norm[i][None], uq,
                                                  mla_kv_norm[i][None], ukk, ukvv, mla_tabs)
            o_a = _na_attention(qa, ka, va, _na_bias(na_rpb[i]))
            o_b = _mla_attention(qm, km, vm)
            xs = _outproj([o_a, o_b], [wo_a, wo_b], xs, g_m, n_tiles)
        else:
            w_in_p, wo = _odd_weights(w_in_odd[i], w_out_odd[i])
            q, kd, vd = _inproj_odd(xs, sh_m, sc_m, w_in_p, swa_tabs)
            sink_col = jnp.repeat(sinks[i].reshape(SWA_KV_HEADS, SWA_GROUP, 1), SWA_BLOCK, axis=1)
            sink_col = sink_col.reshape(SWA_KV_HEADS, SWA_GROUP * SWA_BLOCK, 1)
            n_qblk = N_QBLK_LAT if last else T_ALL // SWA_BLOCK
            o = _swa_attention(q, kd, vd, sink_col, n_qblk)
            xs = _outproj([o], [wo], xs, g_m, n_tiles)
        fw = _ffn_weights(w_up[l], b_up[l], conv_w[l], conv_b[l], w_down[l], b_down[l])
        xs = _ffn(xs, sh_f, sc_f, g_f, *fw, n_tiles)
    return xs
```

```python
import functools

import numpy as np
import jax
import jax.numpy as jnp
from jax import lax
from jax.experimental import pallas as pl
from jax.experimental.pallas import tpu as pltpu

D_MODEL = 1024
BATCH = 16
SEQ = 2048
CTX_LEN = 256
T_ALL = SEQ + CTX_LEN
DEPTH = 4
GRID_W = 64
HEAD_DIM = 64
NA_HEADS = 8
NA_WIN_R = 8
NA_WIN_C = 16
MLA_HEADS = 8
MLA_Q_RANK = 384
MLA_KV_RANK = 256
MLA_NOPE = 64
MLA_ROPE = 32
MLA_V = 64
SWA_Q_HEADS = 16
SWA_KV_HEADS = 2
SWA_WINDOW = 128
SWA_BLOCK = 128
D_FF = 2816
CONV_W = 3
ROPE_BASE = 10000.0
LN_EPS = 1e-6
RMS_EPS = 1e-6
NEG = -1e30
DEEPNORM_ALPHA = (2 * DEPTH) ** 0.25
NA_WIDTH = NA_HEADS * HEAD_DIM

LANES = 128
TM = 256
NT_LAT = SEQ // TM
NT_ALL = T_ALL // TM
HALO = 8
FF_CHUNK = 256
N_FF_CHUNKS = D_FF // FF_CHUNK
NA_QROWS = TM // GRID_W
NA_KROWS = 3 * NA_QROWS
VMEM_LIMIT = 56 * 1024 * 1024
LOG2E = 1.4426950408889634

F32 = jnp.float32
BF16 = jnp.bfloat16


def _params(sem):
    return pltpu.CompilerParams(dimension_semantics=sem, vmem_limit_bytes=VMEM_LIMIT)


def _dot(a, b):
    return jnp.dot(a, b, preferred_element_type=F32)


def _dot_nt(a, b):
    return lax.dot_general(a, b, (((1,), (1,)), ((), ())), preferred_element_type=F32)


def _layer_norm(x):
    mu = jnp.mean(x, axis=-1, keepdims=True)
    xc = x - mu
    var = jnp.mean(xc * xc, axis=-1, keepdims=True)
    return xc * lax.rsqrt(var + LN_EPS)


def _rms_norm(x, g):
    return x * lax.rsqrt(jnp.mean(x * x, axis=-1, keepdims=True) + RMS_EPS) * g


def _silu(x):
    return x / (1.0 + jnp.exp(-x))


def _rope(x, cos, sin, chunk):
    lane = lax.broadcasted_iota(jnp.int32, x.shape, 1)
    first = (lane % (2 * chunk)) < chunk
    rot = jnp.where(first, pltpu.roll(x, LANES - chunk, 1), pltpu.roll(x, chunk, 1))
    return x * cos + rot * sin


def _attend_t(s_list, vt, sink=None):
    m = functools.reduce(jnp.maximum, [jnp.max(s, axis=0, keepdims=True) for s in s_list])
    if sink is not None:
        m = jnp.maximum(m, sink)
    p_list = [jnp.exp2(s - m) for s in s_list]
    l = functools.reduce(jnp.add, [jnp.sum(p, axis=0, keepdims=True) for p in p_list])
    if sink is not None:
        l = l + jnp.exp2(sink - m)
    p_all = jnp.concatenate([p.astype(BF16) for p in p_list], axis=0) if len(p_list) > 1 else p_list[0].astype(BF16)
    return _dot(vt, p_all) * (1.0 / l)


def _low_half(shape):
    return lax.broadcasted_iota(jnp.int32, shape, 1) < HEAD_DIM


def _pack_heads_t(ot0, ot1):
    top = lax.broadcasted_iota(jnp.int32, ot0.shape, 0) < HEAD_DIM
    return jnp.where(top, ot0, ot1).T


def _mod_kernel(c_ref, w_ref, b_ref, o_ref):
    a = _silu(c_ref[...]).astype(BF16)
    o_ref[...] = _dot(a, w_ref[...].astype(BF16)) + b_ref[...]


def _modulation(cs, w_ada, b_ada):
    rows = cs.shape[0]
    tn = 1536
    return pl.pallas_call(
        _mod_kernel,
        grid=(DEPTH, 6 * D_MODEL // tn),
        in_specs=[
            pl.BlockSpec((rows, D_MODEL), lambda l, n: (0, 0)),
            pl.BlockSpec((None, D_MODEL, tn), lambda l, n: (l, 0, n)),
            pl.BlockSpec((None, 1, tn), lambda l, n: (l, 0, n)),
        ],
        out_specs=pl.BlockSpec((None, rows, tn), lambda l, n: (l, 0, n)),
        out_shape=jax.ShapeDtypeStruct((DEPTH, rows, 6 * D_MODEL), F32),
        compiler_params=_params(("arbitrary", "arbitrary")),
        name="adaln_mod",
    )(cs, w_ada, b_ada.reshape(DEPTH, 1, 6 * D_MODEL))


def _mod_spec():
    return pl.BlockSpec((None, 1, D_MODEL), lambda b, t: (2 * b + jnp.where(t >= NT_LAT, 1, 0), 0, 0))


def _const_spec(shape):
    return pl.BlockSpec(shape, lambda *_: (0,) * len(shape))


def _tok_spec(width):
    return pl.BlockSpec((None, TM, width), lambda b, t: (b, t, 0))


def _tok_spec_t(width):
    return pl.BlockSpec((None, width, TM), lambda b, t: (b, 0, t))


def _inproj_even_kernel(x_ref, sh_ref, sc_ref, w_in_ref, qg_ref, w_uq_ref, kg_ref, w_ukk_ref, w_ukv_ref,
                        cq_ref, sq_ref, ck_ref, sk_ref,
                        qa_ref, ka_ref, va_ref, qm_ref, km_ref, vm_ref):
    h = (x_ref[...] * (1.0 + sc_ref[...]) + sh_ref[...]).astype(BF16)
    z = _dot(h, w_in_ref[...])
    w = NA_WIDTH
    qa_ref[...] = (z[:, :w] * (HEAD_DIM ** -0.5 * LOG2E)).astype(BF16)
    ka_ref[...] = z[:, w:2 * w].astype(BF16)
    va_ref[...] = z[:, 2 * w:3 * w].T.astype(BF16)
    o1 = 3 * w
    o2 = o1 + MLA_Q_RANK
    o3 = o2 + MLA_KV_RANK
    cq = _rms_norm(z[:, o1:o2], qg_ref[...]).astype(BF16)
    ckv = _rms_norm(z[:, o2:o3], kg_ref[...]).astype(BF16)
    q = _dot(cq, w_uq_ref[...])
    kn = _dot(ckv, w_ukk_ref[...])
    vm_ref[...] = _dot(ckv, w_ukv_ref[...]).T.astype(BF16)
    kr = _rope(z[:, o3:o3 + LANES], ck_ref[...], sk_ref[...], MLA_ROPE // 4)
    cq_t = cq_ref[...]
    sq_t = sq_ref[...]
    for hh in range(MLA_HEADS):
        sl = slice(hh * LANES, (hh + 1) * LANES)
        qm_ref[:, sl] = _rope(q[:, sl], cq_t, sq_t, MLA_ROPE // 4).astype(BF16)
        km_ref[:, sl] = (kn[:, sl] + kr).astype(BF16)


def _inproj_even(x, sh, sc, w_in, qg, w_uq, kg, w_ukk, w_ukv, tabs):
    cq, sq, ck, sk = tabs
    tab = pl.BlockSpec((TM, LANES), lambda b, t: (t, 0))
    n_in = w_in.shape[1]
    outs = [(NA_WIDTH, False), (NA_WIDTH, False), (NA_WIDTH, True),
            (MLA_HEADS * LANES, False), (MLA_HEADS * LANES, False), (MLA_HEADS * MLA_V, True)]
    return pl.pallas_call(
        _inproj_even_kernel,
        grid=(BATCH, NT_ALL),
        in_specs=[_tok_spec(D_MODEL), _mod_spec(), _mod_spec(),
                  _const_spec((D_MODEL, n_in)),
                  _const_spec((1, MLA_Q_RANK)), _const_spec(w_uq.shape),
                  _const_spec((1, MLA_KV_RANK)), _const_spec(w_ukk.shape), _const_spec(w_ukv.shape),
                  tab, tab, tab, tab],
        out_specs=[_tok_spec_t(wd) if tr else _tok_spec(wd) for wd, tr in outs],
        out_shape=[jax.ShapeDtypeStruct((BATCH, wd, T_ALL) if tr else (BATCH, T_ALL, wd), BF16) for wd, tr in outs],
        compiler_params=_params(("arbitrary", "arbitrary")),
        name="inproj_even",
    )(x, sh, sc, w_in, qg, w_uq, kg, w_ukk, w_ukv, cq, sq, ck, sk)


def _na_kernel(q_ref, k0_ref, k1_ref, k2_ref, kc_ref, v0_ref, v1_ref, v2_ref, vc_ref, bias_ref, o_ref):
    j = pl.program_id(0)
    q = q_ref[...]
    lo = _low_half(q.shape)
    zero = jnp.zeros_like(q)
    qs = jnp.concatenate([jnp.where(lo, q, zero), jnp.where(lo, zero, q)], axis=0)
    n_loc = 3 * TM

    @pl.when(j < NT_LAT)
    def _():
        k_all = jnp.concatenate([k0_ref[...], k1_ref[...], k2_ref[...], kc_ref[...]], axis=0)
        vt_all = jnp.concatenate([v0_ref[...], v1_ref[...], v2_ref[...], vc_ref[...]], axis=1)
        s = _dot_nt(k_all, qs)
        ot = _attend_t([s[:n_loc] + bias_ref[...], s[n_loc:]], vt_all)
        o_ref[...] = _pack_heads_t(ot[:, :TM], ot[:, TM:]).astype(o_ref.dtype)

    @pl.when(j >= NT_LAT)
    def _():
        ot = _attend_t([_dot_nt(kc_ref[...], qs)], vc_ref[...])
        o_ref[...] = _pack_heads_t(ot[:, :TM], ot[:, TM:]).astype(o_ref.dtype)


def _na_attention(qa, ka, vat, bias):
    def kstart(j):
        return jnp.clip(j - 1, 0, NT_LAT - 3)

    def bias_class(j):
        return jnp.where(j == 0, 0, jnp.where(j == NT_LAT - 1, 2, 1))

    blk = (None, TM, LANES)
    blk_t = (None, LANES, TM)
    q_spec = pl.BlockSpec(blk, lambda j, hp, b: (b, j, hp))
    k_specs = [pl.BlockSpec(blk, functools.partial(lambda i, j, hp, b: (b, kstart(j) + i, hp), i))
               for i in range(3)]
    k_specs.append(pl.BlockSpec(blk, lambda j, hp, b: (b, NT_LAT, hp)))
    v_specs = [pl.BlockSpec(blk_t, functools.partial(lambda i, j, hp, b: (b, hp, kstart(j) + i), i))
               for i in range(3)]
    v_specs.append(pl.BlockSpec(blk_t, lambda j, hp, b: (b, hp, NT_LAT)))
    bias_spec = pl.BlockSpec((None, None, 3 * TM, 2 * TM), lambda j, hp, b: (bias_class(j), hp, 0, 0))
    return pl.pallas_call(
        _na_kernel,
        grid=(NT_ALL, NA_HEADS // 2, BATCH),
        in_specs=[q_spec] + k_specs + v_specs + [bias_spec],
        out_specs=q_spec,
        out_shape=jax.ShapeDtypeStruct((BATCH, T_ALL, NA_WIDTH), BF16),
        compiler_params=_params(("arbitrary", "arbitrary", "arbitrary")),
        name="na_attention",
    )(qa, ka, ka, ka, ka, vat, vat, vat, vat, bias)


def _na_bias_tables():
    rows = SEQ // GRID_W
    n_dr, n_dc = 2 * NA_WIN_R - 1, 2 * NA_WIN_C - 1
    kc = np.arange(GRID_W)
    qc = np.arange(GRID_W)
    w_start = np.clip(qc - NA_WIN_C // 2, 0, GRID_W - NA_WIN_C)
    col_ok = (kc[None, :] >= w_start[:, None]) & (kc[None, :] < w_start[:, None] + NA_WIN_C)
    dcol = np.clip(kc[None, :] - qc[:, None] + NA_WIN_C - 1, 0, n_dc - 1)
    col_sel = (dcol[None] == np.arange(n_dc)[:, None, None]).astype(np.float32)
    row_ok, row_sel = [], []
    for j in (0, 1, NT_LAT - 1):
        r = NA_QROWS * j + np.arange(NA_QROWS)
        r0 = np.clip(r - NA_WIN_R // 2, 0, rows - NA_WIN_R)
        start = NA_QROWS * int(np.clip(j - 1, 0, NT_LAT - 3))
        krow = start + np.arange(NA_KROWS)
        row_ok.append((krow[None, :] >= r0[:, None]) & (krow[None, :] < r0[:, None] + NA_WIN_R))
        drow = np.clip(krow[None, :] - r[:, None] + NA_WIN_R - 1, 0, n_dr - 1)
        row_sel.append((drow[..., None] == np.arange(n_dr)).astype(np.float32))
    return col_sel, col_ok, np.stack(row_sel), np.stack(row_ok)


def _na_bias(rpb):
    col_sel, col_ok, row_sel, row_ok = _na_bias_tables()
    hi = lax.Precision.HIGHEST
    by_col = jnp.einsum('hrd,dck->hrkc', rpb, col_sel, precision=hi)
    by_col = by_col.reshape(NA_HEADS // 2, 2, *by_col.shape[1:])
    full = jnp.einsum('samr,perkc->spmkeac', row_sel, by_col, precision=hi)
    ok = (row_ok.transpose(0, 2, 1)[:, None, :, None, None, :, None]
          & col_ok.T[None, None, None, :, None, None, :])
    full = jnp.where(ok, full * LOG2E, NEG)
    return full.reshape(3, NA_HEADS // 2, NA_KROWS * GRID_W, 2 * NA_QROWS * GRID_W)


MLA_KCHUNK = 768


def _mla_kernel(q_ref, k_ref, vt_ref, o_ref):
    jq = pl.program_id(2)
    q = q_ref[...]
    first = lax.broadcasted_iota(jnp.int32, q.shape, 1) < LANES
    zero = jnp.zeros_like(q)
    qs = jnp.concatenate([jnp.where(first, q, zero), jnp.where(first, zero, q)], axis=0)

    def run(k_lo, k_hi):
        s = _dot_nt(k_ref[k_lo:k_hi, :], qs)
        n = k_hi - k_lo
        chunks = [s[c0:min(c0 + MLA_KCHUNK, n)] for c0 in range(0, n, MLA_KCHUNK)]
        ot = _attend_t(chunks, vt_ref[:, k_lo:k_hi])
        o_ref[...] = _pack_heads_t(ot[:, :TM], ot[:, TM:]).astype(o_ref.dtype)

    pl.when(jq < NT_LAT)(lambda: run(0, T_ALL))
    pl.when(jq >= NT_LAT)(lambda: run(SEQ, T_ALL))


def _mla_attention(qm, km, vmt):
    return pl.pallas_call(
        _mla_kernel,
        grid=(BATCH, MLA_HEADS // 2, NT_ALL),
        in_specs=[pl.BlockSpec((None, TM, 2 * LANES), lambda b, hp, j: (b, j, hp)),
                  pl.BlockSpec((None, T_ALL, 2 * LANES), lambda b, hp, j: (b, 0, hp)),
                  pl.BlockSpec((None, LANES, T_ALL), lambda b, hp, j: (b, hp, 0))],
        out_specs=pl.BlockSpec((None, TM, LANES), lambda b, hp, j: (b, j, hp)),
        out_shape=jax.ShapeDtypeStruct((BATCH, T_ALL, MLA_HEADS * MLA_V), BF16),
        compiler_params=_params(("arbitrary", "arbitrary", "arbitrary")),
        name="mla_attention",
    )(qm, km, vmt)


def _inproj_odd_kernel(x_ref, sh_ref, sc_ref, w_in_ref, cq_ref, sq_ref, ck_ref, sk_ref, q_ref, k_ref, v_ref):
    h = (x_ref[...] * (1.0 + sc_ref[...]) + sh_ref[...]).astype(BF16)
    z = _dot(h, w_in_ref[...])
    nq = SWA_Q_HEADS * HEAD_DIM
    nk = 2 * SWA_KV_HEADS * HEAD_DIM
    cq_t, sq_t, ck_t, sk_t = cq_ref[...], sq_ref[...], ck_ref[...], sk_ref[...]
    for g in range(nq // LANES):
        sl = slice(g * LANES, (g + 1) * LANES)
        q_ref[:, sl] = _rope(z[:, sl], cq_t, sq_t, HEAD_DIM // 4).astype(BF16)
    for g in range(nk // LANES):
        sl = slice(g * LANES, (g + 1) * LANES)
        k_ref[:, sl] = _rope(z[:, nq + g * LANES:nq + (g + 1) * LANES], ck_t, sk_t, HEAD_DIM // 4).astype(BF16)
    v_ref[...] = z[:, nq + nk:].T.astype(BF16)


def _inproj_odd(x, sh, sc, w_in, tabs):
    cq, sq, ck, sk = tabs
    tab = pl.BlockSpec((TM, LANES), lambda b, t: (t, 0))
    nq = SWA_Q_HEADS * HEAD_DIM
    nk = 2 * SWA_KV_HEADS * HEAD_DIM
    nv = SWA_KV_HEADS * HEAD_DIM
    return pl.pallas_call(
        _inproj_odd_kernel,
        grid=(BATCH, NT_ALL),
        in_specs=[_tok_spec(D_MODEL), _mod_spec(), _mod_spec(), _const_spec(w_in.shape), tab, tab, tab, tab],
        out_specs=[_tok_spec(nq), _tok_spec(nk), _tok_spec_t(nv)],
        out_shape=[jax.ShapeDtypeStruct((BATCH, T_ALL, nq), BF16), jax.ShapeDtypeStruct((BATCH, T_ALL, nk), BF16),
                   jax.ShapeDtypeStruct((BATCH, nv, T_ALL), BF16)],
        compiler_params=_params(("arbitrary", "arbitrary")),
        name="inproj_odd",
    )(x, sh, sc, w_in, cq, sq, ck, sk)


SWA_GROUP = SWA_Q_HEADS // SWA_KV_HEADS
N_QBLK_LAT = SEQ // SWA_BLOCK


def _swa_kernel(q_ref, k0_ref, k1_ref, k2_ref, kc_ref, v0_ref, v1_ref, v2_ref, vc_ref, band_ref, sink_ref, o_ref):
    n = pl.program_id(2)
    q = q_ref[...]
    lo = _low_half((SWA_BLOCK, LANES))
    parts = []
    for pr in range(SWA_GROUP // 2):
        qp = q[:, pr * LANES:(pr + 1) * LANES]
        zero = jnp.zeros_like(qp)
        parts.append(jnp.where(lo, qp, zero))
        parts.append(jnp.where(lo, zero, qp))
    qs = jnp.concatenate(parts, axis=0)
    k_all = jnp.concatenate([k0_ref[...], k1_ref[...], k2_ref[...], kc_ref[...]], axis=0)
    vt_all = jnp.concatenate([v0_ref[...], v1_ref[...], v2_ref[...], vc_ref[...]], axis=1)
    s = _dot_nt(k_all, qs)
    is_ctx = n >= N_QBLK_LAT
    pen0 = jnp.where((n == 0) | is_ctx, NEG, 0.0)
    pen1 = jnp.where(is_ctx, NEG, 0.0)
    pen2 = jnp.where(n >= N_QBLK_LAT - 1, NEG, 0.0)
    b = SWA_BLOCK
    blocks = [s[:b] + (band_ref[0] + pen0), s[b:2 * b] + pen1, s[2 * b:3 * b] + (band_ref[1] + pen2), s[3 * b:]]
    ot = _attend_t(blocks, vt_all, sink=sink_ref[...])
    for pr in range(SWA_GROUP // 2):
        pair = jnp.concatenate([ot[:, (2 * pr) * b:(2 * pr + 1) * b], ot[:, (2 * pr + 1) * b:(2 * pr + 2) * b]], axis=0)
        o_ref[:, pr * LANES:(pr + 1) * LANES] = pair.T.astype(o_ref.dtype)


def _swa_band():
    kk = np.arange(SWA_BLOCK)[:, None]
    qi = np.arange(SWA_BLOCK)[None, :]
    before = np.where(kk >= qi, 0.0, NEG)
    after = np.where(kk <= qi, 0.0, NEG)
    band = np.stack([before, after]).astype(np.float32)
    return np.tile(band, (1, 1, SWA_GROUP))


def _swa_attention(q, kd, vt, sink_row, n_qblk):
    gw = SWA_GROUP * HEAD_DIM
    rows = SWA_GROUP * SWA_BLOCK
    q_spec = pl.BlockSpec((None, SWA_BLOCK, gw), lambda b, g, n: (b, n, g))

    def near(i):
        return lambda n: jnp.clip(n - 1 + i, 0, N_QBLK_LAT - 1)

    k_specs = [pl.BlockSpec((None, SWA_BLOCK, LANES),
                            functools.partial(lambda f, b, g, n: (b, f(n), g), near(i))) for i in range(3)]
    k_specs.append(pl.BlockSpec((None, CTX_LEN, LANES), lambda b, g, n: (b, SEQ // CTX_LEN, g)))
    v_specs = [pl.BlockSpec((None, HEAD_DIM, SWA_BLOCK),
                            functools.partial(lambda f, b, g, n: (b, g, f(n)), near(i))) for i in range(3)]
    v_specs.append(pl.BlockSpec((None, HEAD_DIM, CTX_LEN), lambda b, g, n: (b, g, SEQ // CTX_LEN)))
    band_spec = _const_spec((2, SWA_BLOCK, rows))
    sink_spec = pl.BlockSpec((None, 1, rows), lambda b, g, n: (g, 0, 0))
    return pl.pallas_call(
        _swa_kernel,
        grid=(BATCH, SWA_KV_HEADS, n_qblk),
        in_specs=[q_spec] + k_specs + v_specs + [band_spec, sink_spec],
        out_specs=q_spec,
        out_shape=jax.ShapeDtypeStruct((BATCH, n_qblk * SWA_BLOCK, SWA_Q_HEADS * HEAD_DIM), BF16),
        compiler_params=_params(("arbitrary", "arbitrary", "arbitrary")),
        name="swa_attention",
    )(q, kd, kd, kd, kd, vt, vt, vt, vt, jnp.asarray(_swa_band()), sink_row)


def _outproj_kernel(n_o, *refs):
    o_refs, w_refs = refs[:n_o], refs[n_o:2 * n_o]
    x_ref, g_ref, out_ref = refs[2 * n_o:]
    y = None
    for o_r, w_r in zip(o_refs, w_refs):
        d = _dot(o_r[...], w_r[...])
        y = d if y is None else y + d
    out_ref[...] = _layer_norm(DEEPNORM_ALPHA * x_ref[...] + g_ref[...] * y)


def _outproj(o_list, w_list, x, gate, n_tiles):
    return pl.pallas_call(
        functools.partial(_outproj_kernel, len(o_list)),
        grid=(BATCH, n_tiles),
        in_specs=[_tok_spec(o.shape[-1]) for o in o_list] + [_const_spec(w.shape) for w in w_list]
                 + [_tok_spec(D_MODEL), _mod_spec()],
        out_specs=_tok_spec(D_MODEL),
        out_shape=jax.ShapeDtypeStruct((BATCH, n_tiles * TM, D_MODEL), F32),
        compiler_params=_params(("arbitrary", "arbitrary")),
        name="outproj_ln",
    )(*o_list, *w_list, x, gate)


def _ffn_kernel(n_tiles, x_ref, xp_ref, xn_ref, sh_ref, sc_ref, g_ref, w_up_ref, b_up_ref, cw_ref, cb_ref,
                w_dn_ref, b_dn_ref, out_ref, act_ref):
    t = pl.program_id(1)
    scale = 1.0 + sc_ref[...]
    shift = sh_ref[...]
    x = x_ref[...]
    rows = TM + 2 * HALO
    hcat = jnp.concatenate([xp_ref[...] * scale + shift, x * scale + shift, xn_ref[...] * scale + shift], axis=0)
    h = hcat.astype(BF16)
    prev_ok = (t != 0) & (t != NT_LAT)
    next_ok = (t != NT_LAT - 1) & (t != n_tiles - 1)
    for c in range(N_FF_CHUNKS):
        b_up = b_up_ref[c]
        cw = cw_ref[c]
        w0, w1, w2 = cw[0:1], cw[1:2], cw[2:3]
        u = _dot(h, w_up_ref[c])
        u = jnp.concatenate([jnp.where(prev_ok, u[:HALO], -b_up), u[HALO:HALO + TM],
                             jnp.where(next_ok, u[HALO + TM:], -b_up)], axis=0)
        cb = cb_ref[c] + b_up * (w0 + w1 + w2)
        v = (w0 * pltpu.roll(u, 1, 0) + w1 * u + w2 * pltpu.roll(u, rows - 1, 0) + cb)[HALO:HALO + TM]
        act_ref[:, c * FF_CHUNK:(c + 1) * FF_CHUNK] = (v[:, :FF_CHUNK] * _silu(v[:, FF_CHUNK:])).astype(BF16)
    y = _dot(act_ref[...], w_dn_ref[...]) + b_dn_ref[...]
    out_ref[...] = _layer_norm(DEEPNORM_ALPHA * x + g_ref[...] * y)


def _ffn(x, sh, sc, gate, w_up, b_up, cw, cb, w_dn, b_dn, n_tiles):
    per_tile = TM // HALO
    last = n_tiles * per_tile - 1
    prev_spec = pl.BlockSpec((None, HALO, D_MODEL), lambda b, t: (b, jnp.maximum(t * per_tile - 1, 0), 0))
    next_spec = pl.BlockSpec((None, HALO, D_MODEL), lambda b, t: (b, jnp.minimum((t + 1) * per_tile, last), 0))
    return pl.pallas_call(
        functools.partial(_ffn_kernel, n_tiles),
        grid=(BATCH, n_tiles),
        in_specs=[_tok_spec(D_MODEL), prev_spec, next_spec, _mod_spec(), _mod_spec(), _mod_spec(),
                  _const_spec(w_up.shape), _const_spec(b_up.shape), _const_spec(cw.shape), _const_spec(cb.shape),
                  _const_spec(w_dn.shape), _const_spec(b_dn.shape)],
        out_specs=_tok_spec(D_MODEL),
        out_shape=jax.ShapeDtypeStruct((BATCH, n_tiles * TM, D_MODEL), F32),
        scratch_shapes=[pltpu.VMEM((TM, D_FF), BF16)],
        compiler_params=_params(("arbitrary", "arbitrary")),
        name="conv_ffn_ln",
    )(x, x, x, sh, sc, gate, w_up, b_up, cw, cb, w_dn, b_dn)


def _rope_tables(rot_dim):
    axis_dim = rot_dim // 2
    t = jnp.arange(SEQ)
    row = (t // GRID_W).astype(F32)[:, None]
    col = (t % GRID_W).astype(F32)[:, None]
    inv_freq = ROPE_BASE ** (-jnp.arange(0, axis_dim, 2, dtype=F32) / axis_dim)
    ar, ac = row * inv_freq, col * inv_freq
    ang = jnp.concatenate([ar, ar, ac, ac], axis=-1)
    sign = np.where(np.arange(rot_dim) % axis_dim < axis_dim // 2, -1.0, 1.0).astype(np.float32)
    return jnp.cos(ang), jnp.sin(ang) * sign


def _with_ctx_rows(cos, sin):
    ones = jnp.ones((CTX_LEN, cos.shape[1]), F32)
    return jnp.concatenate([cos, ones], axis=0), jnp.concatenate([sin, jnp.zeros_like(ones)], axis=0)


def _mla_tables():
    cos, sin = _rope_tables(MLA_ROPE)
    pad_l = jnp.ones((SEQ, MLA_NOPE), F32)
    pad_r = jnp.ones((SEQ, LANES - MLA_NOPE - MLA_ROPE), F32)
    cos = jnp.concatenate([pad_l, cos, pad_r], axis=-1)
    sin = jnp.concatenate([0 * pad_l, sin, 0 * pad_r], axis=-1)
    cos, sin = _with_ctx_rows(cos, sin)
    qs = (MLA_NOPE + MLA_ROPE) ** -0.5 * LOG2E
    return cos * qs, sin * qs, cos, sin


def _swa_tables():
    cos, sin = _rope_tables(HEAD_DIM)
    cos, sin = _with_ctx_rows(jnp.tile(cos, (1, 2)), jnp.tile(sin, (1, 2)))
    qs = HEAD_DIM ** -0.5 * LOG2E
    return cos * qs, sin * qs, cos, sin


def _even_weights(w_in, w_uq, w_ukv, w_out):
    o3 = 3 * NA_WIDTH + MLA_Q_RANK + MLA_KV_RANK
    pad_l = jnp.zeros((D_MODEL, MLA_NOPE), F32)
    pad_r = jnp.zeros((D_MODEL, LANES - MLA_NOPE - MLA_ROPE), F32)
    w_in_p = jnp.concatenate([w_in[:, :o3], pad_l, w_in[:, o3:], pad_r], axis=-1).astype(BF16)
    uq = w_uq.reshape(MLA_Q_RANK, MLA_HEADS, MLA_NOPE + MLA_ROPE)
    uq = jnp.pad(uq, ((0, 0), (0, 0), (0, LANES - MLA_NOPE - MLA_ROPE))).reshape(MLA_Q_RANK, MLA_HEADS * LANES)
    ukv = w_ukv.reshape(MLA_KV_RANK, MLA_HEADS, MLA_NOPE + MLA_V)
    ukk = jnp.pad(ukv[:, :, :MLA_NOPE], ((0, 0), (0, 0), (0, LANES - MLA_NOPE))).reshape(MLA_KV_RANK, MLA_HEADS * LANES)
    ukvv = ukv[:, :, MLA_NOPE:].reshape(MLA_KV_RANK, MLA_HEADS * MLA_V)
    wo = w_out.astype(BF16)
    return w_in_p, uq.astype(BF16), ukk.astype(BF16), ukvv.astype(BF16), wo[:NA_WIDTH], wo[NA_WIDTH:]


def _odd_weights(w_in, w_out):
    nq = SWA_Q_HEADS * HEAD_DIM
    nkv = SWA_KV_HEADS * HEAD_DIM
    dup = lambda w: jnp.repeat(w.reshape(D_MODEL, SWA_KV_HEADS, 1, HEAD_DIM), 2, axis=2).reshape(D_MODEL, 2 * nkv)
    w_in_p = jnp.concatenate([w_in[:, :nq], dup(w_in[:, nq:nq + nkv]), w_in[:, nq + nkv:]], axis=-1)
    return w_in_p.astype(BF16), w_out.astype(BF16)


def _ffn_weights(w_up, b_up, conv_w, conv_b, w_down, b_down):
    def chunked(a):
        lead = a.shape[:-1]
        a = a.reshape(*lead, 2, N_FF_CHUNKS, FF_CHUNK)
        a = jnp.moveaxis(a, -2, 0)
        return a.reshape(N_FF_CHUNKS, *lead, 2 * FF_CHUNK)
    return (chunked(w_up).astype(BF16), chunked(b_up[None]), chunked(conv_w), chunked(conv_b[None]),
            w_down.astype(BF16), b_down[None])


def kernel(x, c, ctx, c_ctx, w_ada, b_ada, na_rpb, w_in_even, mla_q_norm, w_uq, mla_kv_norm, w_ukv, w_out_even,
           w_in_odd, sinks, w_out_odd, w_up, b_up, conv_w, conv_b, w_down, b_down):
    cs = jnp.concatenate([c, c_ctx[None], jnp.zeros((7, D_MODEL), F32)], axis=0)
    mod = _modulation(cs, w_ada, b_ada)
    mod_lat = mod[:, :BATCH].reshape(DEPTH, BATCH, 1, 6, D_MODEL)
    mod_ctx = jnp.broadcast_to(mod[:, BATCH].reshape(DEPTH, 1, 1, 6, D_MODEL), mod_lat.shape)
    mods = jnp.concatenate([mod_lat, mod_ctx], axis=2).reshape(DEPTH, 2 * BATCH, 6, 1, D_MODEL)

    mla_tabs = _mla_tables()
    swa_tabs = _swa_tables()
    xs = jnp.concatenate([x, ctx], axis=1)
    for l in range(DEPTH):
        i = l // 2
        last = l == DEPTH - 1
        n_tiles = NT_LAT if last else NT_ALL
        sh_m, sc_m, g_m, sh_f, sc_f, g_f = (mods[l, :, k] for k in range(6))
        if l % 2 == 0:
            w_in_p, uq, ukk, ukvv, wo_a, wo_b = _even_weights(w_in_even[i], w_uq[i], w_ukv[i], w_out_even[i])
            qa, ka, vat, qm, km, vmt = _inproj_even(xs, sh_m, sc_m, w_in_p, mla_q_norm[i][None], uq,
                                                    mla_kv_norm[i][None], ukk, ukvv, mla_tabs)
            o_a = _na_attention(qa, ka, vat, _na_bias(na_rpb[i]))
            o_b = _mla_attention(qm, km, vmt)
            xs = _outproj([o_a, o_b], [wo_a, wo_b], xs, g_m, n_tiles)
        else:
            w_in_p, wo = _odd_weights(w_in_odd[i], w_out_odd[i])
            q, kd, vt = _inproj_odd(xs, sh_m, sc_m, w_in_p, swa_tabs)
            sink_row = jnp.repeat(sinks[i] * LOG2E, SWA_BLOCK).reshape(SWA_KV_HEADS, 1, SWA_GROUP * SWA_BLOCK)
            n_qblk = N_QBLK_LAT if last else T_ALL // SWA_BLOCK
            o = _swa_attention(q, kd, vt, sink_row, n_qblk)
            xs = _outproj([o], [wo], xs, g_m, n_tiles)
        fw = _ffn_weights(w_up[l], b_up[l], conv_w[l], conv_b[l], w_down[l], b_down[l])
        xs = _ffn(xs, sh_f, sc_f, g_f, *fw, n_tiles)
    return xs
```

```python
import functools

import numpy as np
import jax
import jax.numpy as jnp
from jax import lax
from jax.experimental import pallas as pl
from jax.experimental.pallas import tpu as pltpu

D_MODEL = 1024
BATCH = 16
SEQ = 2048
CTX_LEN = 256
T_ALL = SEQ + CTX_LEN
DEPTH = 4
GRID_W = 64
HEAD_DIM = 64
NA_HEADS = 8
NA_WIN_R = 8
NA_WIN_C = 16
MLA_HEADS = 8
MLA_Q_RANK = 384
MLA_KV_RANK = 256
MLA_NOPE = 64
MLA_ROPE = 32
MLA_V = 64
SWA_Q_HEADS = 16
SWA_KV_HEADS = 2
SWA_WINDOW = 128
SWA_BLOCK = 128
D_FF = 2816
CONV_W = 3
ROPE_BASE = 10000.0
LN_EPS = 1e-6
RMS_EPS = 1e-6
NEG = -1e30
DEEPNORM_ALPHA = (2 * DEPTH) ** 0.25
NA_WIDTH = NA_HEADS * HEAD_DIM

LANES = 128
TM = 256
NT_LAT = SEQ // TM
NT_ALL = T_ALL // TM
HALO = 8
FF_CHUNK = 256
N_FF_CHUNKS = D_FF // FF_CHUNK
NA_QROWS = TM // GRID_W
NA_KROWS = 3 * NA_QROWS
VMEM_LIMIT = 56 * 1024 * 1024
LOG2E = 1.4426950408889634
PAIRS_PER_STEP = 2

F32 = jnp.float32
BF16 = jnp.bfloat16


def _params(sem):
    return pltpu.CompilerParams(dimension_semantics=sem, vmem_limit_bytes=VMEM_LIMIT)


def _dot(a, b):
    return jnp.dot(a, b, preferred_element_type=F32)


def _dot_nt(a, b):
    return lax.dot_general(a, b, (((1,), (1,)), ((), ())), preferred_element_type=F32)


def _layer_norm(x):
    mu = jnp.mean(x, axis=-1, keepdims=True)
    xc = x - mu
    var = jnp.mean(xc * xc, axis=-1, keepdims=True)
    return xc * lax.rsqrt(var + LN_EPS)


def _rms_norm(x, g):
    return x * lax.rsqrt(jnp.mean(x * x, axis=-1, keepdims=True) + RMS_EPS) * g


def _silu(x):
    return x / (1.0 + jnp.exp(-x))


def _rope(x, cos, sin, chunk):
    lane = lax.broadcasted_iota(jnp.int32, x.shape, 1)
    first = (lane % (2 * chunk)) < chunk
    rot = jnp.where(first, pltpu.roll(x, LANES - chunk, 1), pltpu.roll(x, chunk, 1))
    return x * cos + rot * sin


def _attend_t(s_list, vt, sink=None):
    m = functools.reduce(jnp.maximum, [jnp.max(s, axis=0, keepdims=True) for s in s_list])
    if sink is not None:
        m = jnp.maximum(m, sink)
    p_list = [jnp.exp2(s - m) for s in s_list]
    l = functools.reduce(jnp.add, [jnp.sum(p, axis=0, keepdims=True) for p in p_list])
    if sink is not None:
        l = l + jnp.exp2(sink - m)
    p_all = jnp.concatenate([p.astype(BF16) for p in p_list], axis=0) if len(p_list) > 1 else p_list[0].astype(BF16)
    return _dot(vt, p_all) * (1.0 / l)


def _low_half(shape):
    return lax.broadcasted_iota(jnp.int32, shape, 1) < HEAD_DIM


def _pack_heads_t(ot0, ot1):
    top = lax.broadcasted_iota(jnp.int32, ot0.shape, 0) < HEAD_DIM
    return jnp.where(top, ot0, ot1).T


def _mod_kernel(c_ref, w_ref, b_ref, o_ref):
    a = _silu(c_ref[...]).astype(BF16)
    o_ref[...] = _dot(a, w_ref[...].astype(BF16)) + b_ref[...]


def _modulation(cs, w_ada, b_ada):
    rows = cs.shape[0]
    tn = 1536
    return pl.pallas_call(
        _mod_kernel,
        grid=(DEPTH, 6 * D_MODEL // tn),
        in_specs=[
            pl.BlockSpec((rows, D_MODEL), lambda l, n: (0, 0)),
            pl.BlockSpec((None, D_MODEL, tn), lambda l, n: (l, 0, n)),
            pl.BlockSpec((None, 1, tn), lambda l, n: (l, 0, n)),
        ],
        out_specs=pl.BlockSpec((None, rows, tn), lambda l, n: (l, 0, n)),
        out_shape=jax.ShapeDtypeStruct((DEPTH, rows, 6 * D_MODEL), F32),
        compiler_params=_params(("arbitrary", "arbitrary")),
        name="adaln_mod",
    )(cs, w_ada, b_ada.reshape(DEPTH, 1, 6 * D_MODEL))


def _mod_spec():
    return pl.BlockSpec((None, 1, D_MODEL), lambda b, t: (2 * b + jnp.where(t >= NT_LAT, 1, 0), 0, 0))


def _const_spec(shape):
    return pl.BlockSpec(shape, lambda *_: (0,) * len(shape))


def _tok_spec(width):
    return pl.BlockSpec((None, TM, width), lambda b, t: (b, t, 0))


def _tok_spec_t(width):
    return pl.BlockSpec((None, width, TM), lambda b, t: (b, 0, t))


def _inproj_even_kernel(x_ref, sh_ref, sc_ref, w_in_ref, qg_ref, w_uq_ref, kg_ref, w_ukk_ref, w_ukv_ref,
                        cq_ref, sq_ref, ck_ref, sk_ref,
                        qa_ref, ka_ref, va_ref, qm_ref, km_ref, vm_ref):
    h = (x_ref[...] * (1.0 + sc_ref[...]) + sh_ref[...]).astype(BF16)
    z = _dot(h, w_in_ref[...])
    w = NA_WIDTH
    qa_ref[...] = (z[:, :w] * (HEAD_DIM ** -0.5 * LOG2E)).astype(BF16)
    ka_ref[...] = z[:, w:2 * w].astype(BF16)
    va_ref[...] = z[:, 2 * w:3 * w].T.astype(BF16)
    o1 = 3 * w
    o2 = o1 + MLA_Q_RANK
    o3 = o2 + MLA_KV_RANK
    cq = _rms_norm(z[:, o1:o2], qg_ref[...]).astype(BF16)
    ckv = _rms_norm(z[:, o2:o3], kg_ref[...]).astype(BF16)
    q = _dot(cq, w_uq_ref[...])
    kn = _dot(ckv, w_ukk_ref[...])
    vm_ref[...] = _dot(ckv, w_ukv_ref[...]).T.astype(BF16)
    kr = _rope(z[:, o3:o3 + LANES], ck_ref[...], sk_ref[...], MLA_ROPE // 4)
    cq_t = cq_ref[...]
    sq_t = sq_ref[...]
    for hh in range(MLA_HEADS):
        sl = slice(hh * LANES, (hh + 1) * LANES)
        qm_ref[:, sl] = _rope(q[:, sl], cq_t, sq_t, MLA_ROPE // 4).astype(BF16)
        km_ref[:, sl] = (kn[:, sl] + kr).astype(BF16)


def _inproj_even(x, sh, sc, w_in, qg, w_uq, kg, w_ukk, w_ukv, tabs):
    cq, sq, ck, sk = tabs
    tab = pl.BlockSpec((TM, LANES), lambda b, t: (t, 0))
    n_in = w_in.shape[1]
    outs = [(NA_WIDTH, False), (NA_WIDTH, False), (NA_WIDTH, True),
            (MLA_HEADS * LANES, False), (MLA_HEADS * LANES, False), (MLA_HEADS * MLA_V, True)]
    return pl.pallas_call(
        _inproj_even_kernel,
        grid=(BATCH, NT_ALL),
        in_specs=[_tok_spec(D_MODEL), _mod_spec(), _mod_spec(),
                  _const_spec((D_MODEL, n_in)),
                  _const_spec((1, MLA_Q_RANK)), _const_spec(w_uq.shape),
                  _const_spec((1, MLA_KV_RANK)), _const_spec(w_ukk.shape), _const_spec(w_ukv.shape),
                  tab, tab, tab, tab],
        out_specs=[_tok_spec_t(wd) if tr else _tok_spec(wd) for wd, tr in outs],
        out_shape=[jax.ShapeDtypeStruct((BATCH, wd, T_ALL) if tr else (BATCH, T_ALL, wd), BF16) for wd, tr in outs],
        compiler_params=_params(("arbitrary", "arbitrary")),
        name="inproj_even",
    )(x, sh, sc, w_in, qg, w_uq, kg, w_ukk, w_ukv, cq, sq, ck, sk)


def _na_kernel(q_ref, k0_ref, k1_ref, k2_ref, kc_ref, v0_ref, v1_ref, v2_ref, vc_ref, bias_ref, o_ref):
    j = pl.program_id(0)
    lo = _low_half((TM, LANES))
    n_loc = 3 * TM

    def stacked_queries(pp):
        q = q_ref[:, pp * LANES:(pp + 1) * LANES]
        zero = jnp.zeros_like(q)
        return jnp.concatenate([jnp.where(lo, q, zero), jnp.where(lo, zero, q)], axis=0)

    def store(pp, ot):
        o_ref[:, pp * LANES:(pp + 1) * LANES] = _pack_heads_t(ot[:, :TM], ot[:, TM:]).astype(o_ref.dtype)

    @pl.when(j < NT_LAT)
    def _():
        for pp in range(PAIRS_PER_STEP):
            cs = slice(pp * LANES, (pp + 1) * LANES)
            k_all = jnp.concatenate([r[:, cs] for r in (k0_ref, k1_ref, k2_ref, kc_ref)], axis=0)
            vt_all = jnp.concatenate([r[cs, :] for r in (v0_ref, v1_ref, v2_ref, vc_ref)], axis=1)
            s = _dot_nt(k_all, stacked_queries(pp))
            store(pp, _attend_t([s[:n_loc] + bias_ref[pp], s[n_loc:]], vt_all))

    @pl.when(j >= NT_LAT)
    def _():
        for pp in range(PAIRS_PER_STEP):
            cs = slice(pp * LANES, (pp + 1) * LANES)
            store(pp, _attend_t([_dot_nt(kc_ref[:, cs], stacked_queries(pp))], vc_ref[cs, :]))


def _na_attention(qa, ka, vat, bias):
    def kstart(j):
        return jnp.clip(j - 1, 0, NT_LAT - 3)

    def bias_class(j):
        return jnp.where(j == 0, 0, jnp.where(j == NT_LAT - 1, 2, 1))

    pps = PAIRS_PER_STEP
    blk = (None, TM, pps * LANES)
    blk_t = (None, pps * LANES, TM)
    q_spec = pl.BlockSpec(blk, lambda j, hp, b: (b, j, hp))
    k_specs = [pl.BlockSpec(blk, functools.partial(lambda i, j, hp, b: (b, kstart(j) + i, hp), i))
               for i in range(3)]
    k_specs.append(pl.BlockSpec(blk, lambda j, hp, b: (b, NT_LAT, hp)))
    v_specs = [pl.BlockSpec(blk_t, functools.partial(lambda i, j, hp, b: (b, hp, kstart(j) + i), i))
               for i in range(3)]
    v_specs.append(pl.BlockSpec(blk_t, lambda j, hp, b: (b, hp, NT_LAT)))
    bias_spec = pl.BlockSpec((None, pps, 3 * TM, 2 * TM), lambda j, hp, b: (bias_class(j), hp, 0, 0))
    return pl.pallas_call(
        _na_kernel,
        grid=(NT_ALL, NA_HEADS // (2 * pps), BATCH),
        in_specs=[q_spec] + k_specs + v_specs + [bias_spec],
        out_specs=q_spec,
        out_shape=jax.ShapeDtypeStruct((BATCH, T_ALL, NA_WIDTH), BF16),
        compiler_params=_params(("arbitrary", "arbitrary", "arbitrary")),
        name="na_attention",
    )(qa, ka, ka, ka, ka, vat, vat, vat, vat, bias)


def _na_bias_tables():
    rows = SEQ // GRID_W
    n_dr, n_dc = 2 * NA_WIN_R - 1, 2 * NA_WIN_C - 1
    kc = np.arange(GRID_W)
    qc = np.arange(GRID_W)
    w_start = np.clip(qc - NA_WIN_C // 2, 0, GRID_W - NA_WIN_C)
    col_ok = (kc[None, :] >= w_start[:, None]) & (kc[None, :] < w_start[:, None] + NA_WIN_C)
    dcol = np.clip(kc[None, :] - qc[:, None] + NA_WIN_C - 1, 0, n_dc - 1)
    col_sel = (dcol[None] == np.arange(n_dc)[:, None, None]).astype(np.float32)
    row_ok, row_sel = [], []
    for j in (0, 1, NT_LAT - 1):
        r = NA_QROWS * j + np.arange(NA_QROWS)
        r0 = np.clip(r - NA_WIN_R // 2, 0, rows - NA_WIN_R)
        start = NA_QROWS * int(np.clip(j - 1, 0, NT_LAT - 3))
        krow = start + np.arange(NA_KROWS)
        row_ok.append((krow[None, :] >= r0[:, None]) & (krow[None, :] < r0[:, None] + NA_WIN_R))
        drow = np.clip(krow[None, :] - r[:, None] + NA_WIN_R - 1, 0, n_dr - 1)
        row_sel.append((drow[..., None] == np.arange(n_dr)).astype(np.float32))
    return col_sel, col_ok, np.stack(row_sel), np.stack(row_ok)


def _na_bias(rpb):
    col_sel, col_ok, row_sel, row_ok = _na_bias_tables()
    hi = lax.Precision.HIGHEST
    by_col = jnp.einsum('hrd,dck->hrkc', rpb, col_sel, precision=hi)
    by_col = by_col.reshape(NA_HEADS // 2, 2, *by_col.shape[1:])
    full = jnp.einsum('samr,perkc->spmkeac', row_sel, by_col, precision=hi)
    ok = (row_ok.transpose(0, 2, 1)[:, None, :, None, None, :, None]
          & col_ok.T[None, None, None, :, None, None, :])
    full = jnp.where(ok, full * LOG2E, NEG)
    return full.reshape(3, NA_HEADS // 2, NA_KROWS * GRID_W, 2 * NA_QROWS * GRID_W)


MLA_KCHUNK = 768


def _mla_kernel(q_ref, k_ref, vt_ref, o_ref):
    jq = pl.program_id(2)
    first = lax.broadcasted_iota(jnp.int32, (TM, 2 * LANES), 1) < LANES

    def run(k_lo, k_hi):
        for pp in range(PAIRS_PER_STEP):
            cs = slice(pp * 2 * LANES, (pp + 1) * 2 * LANES)
            q = q_ref[:, cs]
            zero = jnp.zeros_like(q)
            qs = jnp.concatenate([jnp.where(first, q, zero), jnp.where(first, zero, q)], axis=0)
            s = _dot_nt(k_ref[k_lo:k_hi, cs], qs)
            n = k_hi - k_lo
            chunks = [s[c0:min(c0 + MLA_KCHUNK, n)] for c0 in range(0, n, MLA_KCHUNK)]
            ot = _attend_t(chunks, vt_ref[pp * LANES:(pp + 1) * LANES, k_lo:k_hi])
            o_ref[:, pp * LANES:(pp + 1) * LANES] = _pack_heads_t(ot[:, :TM], ot[:, TM:]).astype(o_ref.dtype)

    pl.when(jq < NT_LAT)(lambda: run(0, T_ALL))
    pl.when(jq >= NT_LAT)(lambda: run(SEQ, T_ALL))


def _mla_attention(qm, km, vmt):
    pps = PAIRS_PER_STEP
    return pl.pallas_call(
        _mla_kernel,
        grid=(BATCH, MLA_HEADS // (2 * pps), NT_ALL),
        in_specs=[pl.BlockSpec((None, TM, pps * 2 * LANES), lambda b, hp, j: (b, j, hp)),
                  pl.BlockSpec((None, T_ALL, pps * 2 * LANES), lambda b, hp, j: (b, 0, hp)),
                  pl.BlockSpec((None, pps * LANES, T_ALL), lambda b, hp, j: (b, hp, 0))],
        out_specs=pl.BlockSpec((None, TM, pps * LANES), lambda b, hp, j: (b, j, hp)),
        out_shape=jax.ShapeDtypeStruct((BATCH, T_ALL, MLA_HEADS * MLA_V), BF16),
        compiler_params=_params(("arbitrary", "arbitrary", "arbitrary")),
        name="mla_attention",
    )(qm, km, vmt)


def _inproj_odd_kernel(x_ref, sh_ref, sc_ref, w_in_ref, cq_ref, sq_ref, ck_ref, sk_ref, q_ref, k_ref, v_ref):
    h = (x_ref[...] * (1.0 + sc_ref[...]) + sh_ref[...]).astype(BF16)
    z = _dot(h, w_in_ref[...])
    nq = SWA_Q_HEADS * HEAD_DIM
    nk = 2 * SWA_KV_HEADS * HEAD_DIM
    cq_t, sq_t, ck_t, sk_t = cq_ref[...], sq_ref[...], ck_ref[...], sk_ref[...]
    for g in range(nq // LANES):
        sl = slice(g * LANES, (g + 1) * LANES)
        q_ref[:, sl] = _rope(z[:, sl], cq_t, sq_t, HEAD_DIM // 4).astype(BF16)
    for g in range(nk // LANES):
        sl = slice(g * LANES, (g + 1) * LANES)
        k_ref[:, sl] = _rope(z[:, nq + g * LANES:nq + (g + 1) * LANES], ck_t, sk_t, HEAD_DIM // 4).astype(BF16)
    v_ref[...] = z[:, nq + nk:].T.astype(BF16)


def _inproj_odd(x, sh, sc, w_in, tabs):
    cq, sq, ck, sk = tabs
    tab = pl.BlockSpec((TM, LANES), lambda b, t: (t, 0))
    nq = SWA_Q_HEADS * HEAD_DIM
    nk = 2 * SWA_KV_HEADS * HEAD_DIM
    nv = SWA_KV_HEADS * HEAD_DIM
    return pl.pallas_call(
        _inproj_odd_kernel,
        grid=(BATCH, NT_ALL),
        in_specs=[_tok_spec(D_MODEL), _mod_spec(), _mod_spec(), _const_spec(w_in.shape), tab, tab, tab, tab],
        out_specs=[_tok_spec(nq), _tok_spec(nk), _tok_spec_t(nv)],
        out_shape=[jax.ShapeDtypeStruct((BATCH, T_ALL, nq), BF16), jax.ShapeDtypeStruct((BATCH, T_ALL, nk), BF16),
                   jax.ShapeDtypeStruct((BATCH, nv, T_ALL), BF16)],
        compiler_params=_params(("arbitrary", "arbitrary")),
        name="inproj_odd",
    )(x, sh, sc, w_in, cq, sq, ck, sk)


SWA_GROUP = SWA_Q_HEADS // SWA_KV_HEADS
N_QBLK_LAT = SEQ // SWA_BLOCK


def _swa_kernel(q_ref, k0_ref, k1_ref, k2_ref, kc_ref, v0_ref, v1_ref, v2_ref, vc_ref, band_ref, sink_ref, o_ref):
    n = pl.program_id(1)
    lo = _low_half((SWA_BLOCK, LANES))
    is_ctx = n >= N_QBLK_LAT
    pen0 = jnp.where((n == 0) | is_ctx, NEG, 0.0)
    pen1 = jnp.where(is_ctx, NEG, 0.0)
    pen2 = jnp.where(n >= N_QBLK_LAT - 1, NEG, 0.0)
    mask0 = band_ref[0] + pen0
    mask2 = band_ref[1] + pen2
    b = SWA_BLOCK
    gw = SWA_GROUP * HEAD_DIM
    for g in range(SWA_KV_HEADS):
        parts = []
        for pr in range(SWA_GROUP // 2):
            qp = q_ref[:, g * gw + pr * LANES:g * gw + (pr + 1) * LANES]
            zero = jnp.zeros_like(qp)
            parts.append(jnp.where(lo, qp, zero))
            parts.append(jnp.where(lo, zero, qp))
        qs = jnp.concatenate(parts, axis=0)
        ks = slice(g * LANES, (g + 1) * LANES)
        vs = slice(g * HEAD_DIM, (g + 1) * HEAD_DIM)
        k_all = jnp.concatenate([r[:, ks] for r in (k0_ref, k1_ref, k2_ref, kc_ref)], axis=0)
        vt_all = jnp.concatenate([r[vs, :] for r in (v0_ref, v1_ref, v2_ref, vc_ref)], axis=1)
        s = _dot_nt(k_all, qs)
        blocks = [s[:b] + mask0, s[b:2 * b] + pen1, s[2 * b:3 * b] + mask2, s[3 * b:]]
        ot = _attend_t(blocks, vt_all, sink=sink_ref[g])
        for pr in range(SWA_GROUP // 2):
            pair = jnp.concatenate([ot[:, (2 * pr) * b:(2 * pr + 1) * b], ot[:, (2 * pr + 1) * b:(2 * pr + 2) * b]],
                                   axis=0)
            o_ref[:, g * gw + pr * LANES:g * gw + (pr + 1) * LANES] = pair.T.astype(o_ref.dtype)


def _swa_band():
    kk = np.arange(SWA_BLOCK)[:, None]
    qi = np.arange(SWA_BLOCK)[None, :]
    before = np.where(kk >= qi, 0.0, NEG)
    after = np.where(kk <= qi, 0.0, NEG)
    band = np.stack([before, after]).astype(np.float32)
    return np.tile(band, (1, 1, SWA_GROUP))


def _swa_attention(q, kd, vt, sink_row, n_qblk):
    rows = SWA_GROUP * SWA_BLOCK
    kw = SWA_KV_HEADS * LANES
    vw = SWA_KV_HEADS * HEAD_DIM
    q_spec = pl.BlockSpec((None, SWA_BLOCK, SWA_Q_HEADS * HEAD_DIM), lambda b, n: (b, n, 0))

    def near(i):
        return lambda n: jnp.clip(n - 1 + i, 0, N_QBLK_LAT - 1)

    k_specs = [pl.BlockSpec((None, SWA_BLOCK, kw), functools.partial(lambda f, b, n: (b, f(n), 0), near(i)))
               for i in range(3)]
    k_specs.append(pl.BlockSpec((None, CTX_LEN, kw), lambda b, n: (b, SEQ // CTX_LEN, 0)))
    v_specs = [pl.BlockSpec((None, vw, SWA_BLOCK), functools.partial(lambda f, b, n: (b, 0, f(n)), near(i)))
               for i in range(3)]
    v_specs.append(pl.BlockSpec((None, vw, CTX_LEN), lambda b, n: (b, 0, SEQ // CTX_LEN)))
    band_spec = _const_spec((2, SWA_BLOCK, rows))
    sink_spec = _const_spec((SWA_KV_HEADS, 1, rows))
    return pl.pallas_call(
        _swa_kernel,
        grid=(BATCH, n_qblk),
        in_specs=[q_spec] + k_specs + v_specs + [band_spec, sink_spec],
        out_specs=q_spec,
        out_shape=jax.ShapeDtypeStruct((BATCH, n_qblk * SWA_BLOCK, SWA_Q_HEADS * HEAD_DIM), BF16),
        compiler_params=_params(("arbitrary", "arbitrary")),
        name="swa_attention",
    )(q, kd, kd, kd, kd, vt, vt, vt, vt, jnp.asarray(_swa_band()), sink_row)


def _outproj_kernel(n_o, *refs):
    o_refs, w_refs = refs[:n_o], refs[n_o:2 * n_o]
    x_ref, g_ref, out_ref = refs[2 * n_o:]
    y = None
    for o_r, w_r in zip(o_refs, w_refs):
        d = _dot(o_r[...], w_r[...])
        y = d if y is None else y + d
    out_ref[...] = _layer_norm(DEEPNORM_ALPHA * x_ref[...] + g_ref[...] * y)


def _outproj(o_list, w_list, x, gate, n_tiles):
    return pl.pallas_call(
        functools.partial(_outproj_kernel, len(o_list)),
        grid=(BATCH, n_tiles),
        in_specs=[_tok_spec(o.shape[-1]) for o in o_list] + [_const_spec(w.shape) for w in w_list]
                 + [_tok_spec(D_MODEL), _mod_spec()],
        out_specs=_tok_spec(D_MODEL),
        out_shape=jax.ShapeDtypeStruct((BATCH, n_tiles * TM, D_MODEL), F32),
        compiler_params=_params(("arbitrary", "arbitrary")),
        name="outproj_ln",
    )(*o_list, *w_list, x, gate)


def _ffn_kernel(n_tiles, x_ref, xp_ref, xn_ref, sh_ref, sc_ref, g_ref, w_up_ref, b_up_ref, cw_ref, cb_ref,
                w_dn_ref, b_dn_ref, out_ref, act_ref):
    t = pl.program_id(1)
    scale = 1.0 + sc_ref[...]
    shift = sh_ref[...]
    x = x_ref[...]
    rows = TM + 2 * HALO
    hcat = jnp.concatenate([xp_ref[...] * scale + shift, x * scale + shift, xn_ref[...] * scale + shift], axis=0)
    h = hcat.astype(BF16)
    prev_ok = (t != 0) & (t != NT_LAT)
    next_ok = (t != NT_LAT - 1) & (t != n_tiles - 1)
    for c in range(N_FF_CHUNKS):
        b_up = b_up_ref[c]
        cw = cw_ref[c]
        w0, w1, w2 = cw[0:1], cw[1:2], cw[2:3]
        u = _dot(h, w_up_ref[c])
        u = jnp.concatenate([jnp.where(prev_ok, u[:HALO], -b_up), u[HALO:HALO + TM],
                             jnp.where(next_ok, u[HALO + TM:], -b_up)], axis=0)
        cb = cb_ref[c] + b_up * (w0 + w1 + w2)
        v = (w0 * pltpu.roll(u, 1, 0) + w1 * u + w2 * pltpu.roll(u, rows - 1, 0) + cb)[HALO:HALO + TM]
        act_ref[:, c * FF_CHUNK:(c + 1) * FF_CHUNK] = (v[:, :FF_CHUNK] * _silu(v[:, FF_CHUNK:])).astype(BF16)
    y = _dot(act_ref[...], w_dn_ref[...]) + b_dn_ref[...]
    out_ref[...] = _layer_norm(DEEPNORM_ALPHA * x + g_ref[...] * y)


def _ffn(x, sh, sc, gate, w_up, b_up, cw, cb, w_dn, b_dn, n_tiles):
    per_tile = TM // HALO
    last = n_tiles * per_tile - 1
    prev_spec = pl.BlockSpec((None, HALO, D_MODEL), lambda b, t: (b, jnp.maximum(t * per_tile - 1, 0), 0))
    next_spec = pl.BlockSpec((None, HALO, D_MODEL), lambda b, t: (b, jnp.minimum((t + 1) * per_tile, last), 0))
    return pl.pallas_call(
        functools.partial(_ffn_kernel, n_tiles),
        grid=(BATCH, n_tiles),
        in_specs=[_tok_spec(D_MODEL), prev_spec, next_spec, _mod_spec(), _mod_spec(), _mod_spec(),
                  _const_spec(w_up.shape), _const_spec(b_up.shape), _const_spec(cw.shape), _const_spec(cb.shape),
                  _const_spec(w_dn.shape), _const_spec(b_dn.shape)],
        out_specs=_tok_spec(D_MODEL),
        out_shape=jax.ShapeDtypeStruct((BATCH, n_tiles * TM, D_MODEL), F32),
        scratch_shapes=[pltpu.VMEM((TM, D_FF), BF16)],
        compiler_params=_params(("arbitrary", "arbitrary")),
        name="conv_ffn_ln",
    )(x, x, x, sh, sc, gate, w_up, b_up, cw, cb, w_dn, b_dn)


def _rope_tables(rot_dim):
    axis_dim = rot_dim // 2
    t = jnp.arange(SEQ)
    row = (t // GRID_W).astype(F32)[:, None]
    col = (t % GRID_W).astype(F32)[:, None]
    inv_freq = ROPE_BASE ** (-jnp.arange(0, axis_dim, 2, dtype=F32) / axis_dim)
    ar, ac = row * inv_freq, col * inv_freq
    ang = jnp.concatenate([ar, ar, ac, ac], axis=-1)
    sign = np.where(np.arange(rot_dim) % axis_dim < axis_dim // 2, -1.0, 1.0).astype(np.float32)
    return jnp.cos(ang), jnp.sin(ang) * sign


def _with_ctx_rows(cos, sin):
    ones = jnp.ones((CTX_LEN, cos.shape[1]), F32)
    return jnp.concatenate([cos, ones], axis=0), jnp.concatenate([sin, jnp.zeros_like(ones)], axis=0)


def _mla_tables():
    cos, sin = _rope_tables(MLA_ROPE)
    pad_l = jnp.ones((SEQ, MLA_NOPE), F32)
    pad_r = jnp.ones((SEQ, LANES - MLA_NOPE - MLA_ROPE), F32)
    cos = jnp.concatenate([pad_l, cos, pad_r], axis=-1)
    sin = jnp.concatenate([0 * pad_l, sin, 0 * pad_r], axis=-1)
    cos, sin = _with_ctx_rows(cos, sin)
    qs = (MLA_NOPE + MLA_ROPE) ** -0.5 * LOG2E
    return cos * qs, sin * qs, cos, sin


def _swa_tables():
    cos, sin = _rope_tables(HEAD_DIM)
    cos, sin = _with_ctx_rows(jnp.tile(cos, (1, 2)), jnp.tile(sin, (1, 2)))
    qs = HEAD_DIM ** -0.5 * LOG2E
    return cos * qs, sin * qs, cos, sin


def _even_weights(w_in, w_uq, w_ukv, w_out):
    o3 = 3 * NA_WIDTH + MLA_Q_RANK + MLA_KV_RANK
    pad_l = jnp.zeros((D_MODEL, MLA_NOPE), F32)
    pad_r = jnp.zeros((D_MODEL, LANES - MLA_NOPE - MLA_ROPE), F32)
    w_in_p = jnp.concatenate([w_in[:, :o3], pad_l, w_in[:, o3:], pad_r], axis=-1).astype(BF16)
    uq = w_uq.reshape(MLA_Q_RANK, MLA_HEADS, MLA_NOPE + MLA_ROPE)
    uq = jnp.pad(uq, ((0, 0), (0, 0), (0, LANES - MLA_NOPE - MLA_ROPE))).reshape(MLA_Q_RANK, MLA_HEADS * LANES)
    ukv = w_ukv.reshape(MLA_KV_RANK, MLA_HEADS, MLA_NOPE + MLA_V)
    ukk = jnp.pad(ukv[:, :, :MLA_NOPE], ((0, 0), (0, 0), (0, LANES - MLA_NOPE))).reshape(MLA_KV_RANK, MLA_HEADS * LANES)
    ukvv = ukv[:, :, MLA_NOPE:].reshape(MLA_KV_RANK, MLA_HEADS * MLA_V)
    wo = w_out.astype(BF16)
    return w_in_p, uq.astype(BF16), ukk.astype(BF16), ukvv.astype(BF16), wo[:NA_WIDTH], wo[NA_WIDTH:]


def _odd_weights(w_in, w_out):
    nq = SWA_Q_HEADS * HEAD_DIM
    nkv = SWA_KV_HEADS * HEAD_DIM
    dup = lambda w: jnp.repeat(w.reshape(D_MODEL, SWA_KV_HEADS, 1, HEAD_DIM), 2, axis=2).reshape(D_MODEL, 2 * nkv)
    w_in_p = jnp.concatenate([w_in[:, :nq], dup(w_in[:, nq:nq + nkv]), w_in[:, nq + nkv:]], axis=-1)
    return w_in_p.astype(BF16), w_out.astype(BF16)


def _ffn_weights(w_up, b_up, conv_w, conv_b, w_down, b_down):
    def chunked(a):
        lead = a.shape[:-1]
        a = a.reshape(*lead, 2, N_FF_CHUNKS, FF_CHUNK)
        a = jnp.moveaxis(a, -2, 0)
        return a.reshape(N_FF_CHUNKS, *lead, 2 * FF_CHUNK)
    return (chunked(w_up).astype(BF16), chunked(b_up[None]), chunked(conv_w), chunked(conv_b[None]),
            w_down.astype(BF16), b_down[None])


def kernel(x, c, ctx, c_ctx, w_ada, b_ada, na_rpb, w_in_even, mla_q_norm, w_uq, mla_kv_norm, w_ukv, w_out_even,
           w_in_odd, sinks, w_out_odd, w_up, b_up, conv_w, conv_b, w_down, b_down):
    cs = jnp.concatenate([c, c_ctx[None], jnp.zeros((7, D_MODEL), F32)], axis=0)
    mod = _modulation(cs, w_ada, b_ada)
    mod_lat = mod[:, :BATCH].reshape(DEPTH, BATCH, 1, 6, D_MODEL)
    mod_ctx = jnp.broadcast_to(mod[:, BATCH].reshape(DEPTH, 1, 1, 6, D_MODEL), mod_lat.shape)
    mods = jnp.concatenate([mod_lat, mod_ctx], axis=2).reshape(DEPTH, 2 * BATCH, 6, 1, D_MODEL)

    mla_tabs = _mla_tables()
    swa_tabs = _swa_tables()
    xs = jnp.concatenate([x, ctx], axis=1)
    for l in range(DEPTH):
        i = l // 2
        last = l == DEPTH - 1
        n_tiles = NT_LAT if last else NT_ALL
        sh_m, sc_m, g_m, sh_f, sc_f, g_f = (mods[l, :, k] for k in range(6))
        if l % 2 == 0:
            w_in_p, uq, ukk, ukvv, wo_a, wo_b = _even_weights(w_in_even[i], w_uq[i], w_ukv[i], w_out_even[i])
            qa, ka, vat, qm, km, vmt = _inproj_even(xs, sh_m, sc_m, w_in_p, mla_q_norm[i][None], uq,
                                                    mla_kv_norm[i][None], ukk, ukvv, mla_tabs)
            o_a = _na_attention(qa, ka, vat, _na_bias(na_rpb[i]))
            o_b = _mla_attention(qm, km, vmt)
            xs = _outproj([o_a, o_b], [wo_a, wo_b], xs, g_m, n_tiles)
        else:
            w_in_p, wo = _odd_weights(w_in_odd[i], w_out_odd[i])
            q, kd, vt = _inproj_odd(xs, sh_m, sc_m, w_in_p, swa_tabs)
            sink_row = jnp.repeat(sinks[i] * LOG2E, SWA_BLOCK).reshape(SWA_KV_HEADS, 1, SWA_GROUP * SWA_BLOCK)
            n_qblk = N_QBLK_LAT if last else T_ALL // SWA_BLOCK
            o = _swa_attention(q, kd, vt, sink_row, n_qblk)
            xs = _outproj([o], [wo], xs, g_m, n_tiles)
        fw = _ffn_weights(w_up[l], b_up[l], conv_w[l], conv_b[l], w_down[l], b_down[l])
        xs = _ffn(xs, sh_f, sc_f, g_f, *fw, n_tiles)
    return xs
```

```python
import functools

import numpy as np
import jax
import jax.numpy as jnp
from jax import lax
from jax.experimental import pallas as pl
from jax.experimental.pallas import tpu as pltpu

D_MODEL = 1024
BATCH = 16
SEQ = 2048
CTX_LEN = 256
T_ALL = SEQ + CTX_LEN
DEPTH = 4
GRID_W = 64
HEAD_DIM = 64
NA_HEADS = 8
NA_WIN_R = 8
NA_WIN_C = 16
MLA_HEADS = 8
MLA_Q_RANK = 384
MLA_KV_RANK = 256
MLA_NOPE = 64
MLA_ROPE = 32
MLA_V = 64
SWA_Q_HEADS = 16
SWA_KV_HEADS = 2
SWA_WINDOW = 128
SWA_BLOCK = 128
D_FF = 2816
CONV_W = 3
ROPE_BASE = 10000.0
LN_EPS = 1e-6
RMS_EPS = 1e-6
NEG = -1e30
DEEPNORM_ALPHA = (2 * DEPTH) ** 0.25
NA_WIDTH = NA_HEADS * HEAD_DIM

LANES = 128
TM = 256
NT_LAT = SEQ // TM
NT_ALL = T_ALL // TM
HALO = 8
NBR = 16
FF_CHUNK = 256
N_FF_CHUNKS = D_FF // FF_CHUNK
NA_QROWS = TM // GRID_W
NA_KROWS = 3 * NA_QROWS
VMEM_LIMIT = 56 * 1024 * 1024
LOG2E = 1.4426950408889634
PAIRS_PER_STEP = 4

F32 = jnp.float32
BF16 = jnp.bfloat16


def _params(sem):
    return pltpu.CompilerParams(dimension_semantics=sem, vmem_limit_bytes=VMEM_LIMIT)


def _dot(a, b):
    return jnp.dot(a, b, preferred_element_type=F32)


def _dot_nt(a, b):
    return lax.dot_general(a, b, (((1,), (1,)), ((), ())), preferred_element_type=F32)


def _layer_norm(x):
    mu = jnp.mean(x, axis=-1, keepdims=True)
    xc = x - mu
    var = jnp.mean(xc * xc, axis=-1, keepdims=True)
    return xc * lax.rsqrt(var + LN_EPS)


def _rms_norm(x, g):
    return x * lax.rsqrt(jnp.mean(x * x, axis=-1, keepdims=True) + RMS_EPS) * g


def _silu(x):
    return x / (1.0 + jnp.exp(-x))


def _rope(x, cos, sin, chunk):
    lane = lax.broadcasted_iota(jnp.int32, x.shape, 1)
    first = (lane % (2 * chunk)) < chunk
    rot = jnp.where(first, pltpu.roll(x, LANES - chunk, 1), pltpu.roll(x, chunk, 1))
    return x * cos + rot * sin


def _attend_t(s_list, vt, sink=None):
    m = l = o = None
    k0 = 0
    for s in s_list:
        k1 = k0 + s.shape[0]
        m_blk = jnp.max(s, axis=0, keepdims=True)
        m_new = m_blk if m is None else jnp.maximum(m, m_blk)
        p = jnp.exp2(s - m_new)
        p_sum = jnp.sum(p, axis=0, keepdims=True)
        pv = _dot(vt[:, k0:k1], p.astype(BF16))
        if m is None:
            l, o = p_sum, pv
        else:
            alpha = jnp.exp2(m - m_new)
            l = l * alpha + p_sum
            o = o * alpha + pv
        m, k0 = m_new, k1
    if sink is not None:
        m_new = jnp.maximum(m, sink)
        alpha = jnp.exp2(m - m_new)
        l = l * alpha + jnp.exp2(sink - m_new)
        o = o * alpha
    return o * (1.0 / l)


def _low_half(shape):
    return lax.broadcasted_iota(jnp.int32, shape, 1) < HEAD_DIM


def _pack_heads_t(ot0, ot1):
    top = lax.broadcasted_iota(jnp.int32, ot0.shape, 0) < HEAD_DIM
    return jnp.where(top, ot0, ot1).T


def _mod_kernel(c_ref, w_ref, b_ref, o_ref):
    a = _silu(c_ref[...]).astype(BF16)
    o_ref[...] = _dot(a, w_ref[...].astype(BF16)) + b_ref[...]


def _modulation(cs, w_ada, b_ada):
    rows = cs.shape[0]
    tn = 1536
    return pl.pallas_call(
        _mod_kernel,
        grid=(DEPTH, 6 * D_MODEL // tn),
        in_specs=[
            pl.BlockSpec((rows, D_MODEL), lambda l, n: (0, 0)),
            pl.BlockSpec((None, D_MODEL, tn), lambda l, n: (l, 0, n)),
            pl.BlockSpec((None, 1, tn), lambda l, n: (l, 0, n)),
        ],
        out_specs=pl.BlockSpec((None, rows, tn), lambda l, n: (l, 0, n)),
        out_shape=jax.ShapeDtypeStruct((DEPTH, rows, 6 * D_MODEL), F32),
        compiler_params=_params(("arbitrary", "arbitrary")),
        name="adaln_mod",
    )(cs, w_ada, b_ada.reshape(DEPTH, 1, 6 * D_MODEL))


def _mod_spec():
    return pl.BlockSpec((None, 1, D_MODEL), lambda b, t: (2 * b + jnp.where(t >= NT_LAT, 1, 0), 0, 0))


def _const_spec(shape):
    return pl.BlockSpec(shape, lambda *_: (0,) * len(shape))


def _tok_spec(width):
    return pl.BlockSpec((None, TM, width), lambda b, t: (b, t, 0))


def _tok_spec_t(width):
    return pl.BlockSpec((None, width, TM), lambda b, t: (b, 0, t))


def _inproj_even_kernel(x_ref, sh_ref, sc_ref, w_in_ref, qg_ref, w_uq_ref, kg_ref, w_ukk_ref, w_ukv_ref,
                        cq_ref, sq_ref, ck_ref, sk_ref,
                        qa_ref, ka_ref, va_ref, qm_ref, km_ref, vm_ref):
    h = (x_ref[...] * (1.0 + sc_ref[...]) + sh_ref[...]).astype(BF16)
    z = _dot(h, w_in_ref[...])
    w = NA_WIDTH
    qa_ref[...] = (z[:, :w] * (HEAD_DIM ** -0.5 * LOG2E)).astype(BF16)
    ka_ref[...] = z[:, w:2 * w].astype(BF16)
    va_ref[...] = z[:, 2 * w:3 * w].T.astype(BF16)
    o1 = 3 * w
    o2 = o1 + MLA_Q_RANK
    o3 = o2 + MLA_KV_RANK
    cq = _rms_norm(z[:, o1:o2], qg_ref[...]).astype(BF16)
    ckv = _rms_norm(z[:, o2:o3], kg_ref[...]).astype(BF16)
    q = _dot(cq, w_uq_ref[...])
    kn = _dot(ckv, w_ukk_ref[...])
    vm_ref[...] = _dot(ckv, w_ukv_ref[...]).T.astype(BF16)
    kr = _rope(z[:, o3:o3 + LANES], ck_ref[...], sk_ref[...], MLA_ROPE // 4)
    cq_t = cq_ref[...]
    sq_t = sq_ref[...]
    for hh in range(MLA_HEADS):
        sl = slice(hh * LANES, (hh + 1) * LANES)
        qm_ref[:, sl] = _rope(q[:, sl], cq_t, sq_t, MLA_ROPE // 4).astype(BF16)
        km_ref[:, sl] = (kn[:, sl] + kr).astype(BF16)


def _inproj_even(x, sh, sc, w_in, qg, w_uq, kg, w_ukk, w_ukv, tabs):
    cq, sq, ck, sk = tabs
    tab = pl.BlockSpec((TM, LANES), lambda b, t: (t, 0))
    n_in = w_in.shape[1]
    outs = [(NA_WIDTH, False), (NA_WIDTH, False), (NA_WIDTH, True),
            (MLA_HEADS * LANES, False), (MLA_HEADS * LANES, False), (MLA_HEADS * MLA_V, True)]
    return pl.pallas_call(
        _inproj_even_kernel,
        grid=(BATCH, NT_ALL),
        in_specs=[_tok_spec(D_MODEL), _mod_spec(), _mod_spec(),
                  _const_spec((D_MODEL, n_in)),
                  _const_spec((1, MLA_Q_RANK)), _const_spec(w_uq.shape),
                  _const_spec((1, MLA_KV_RANK)), _const_spec(w_ukk.shape), _const_spec(w_ukv.shape),
                  tab, tab, tab, tab],
        out_specs=[_tok_spec_t(wd) if tr else _tok_spec(wd) for wd, tr in outs],
        out_shape=[jax.ShapeDtypeStruct((BATCH, wd, T_ALL) if tr else (BATCH, T_ALL, wd), BF16) for wd, tr in outs],
        compiler_params=_params(("arbitrary", "arbitrary")),
        name="inproj_even",
    )(x, sh, sc, w_in, qg, w_uq, kg, w_ukk, w_ukv, cq, sq, ck, sk)


def _na_kernel(q_ref, k0_ref, k1_ref, k2_ref, kc_ref, v0_ref, v1_ref, v2_ref, vc_ref, bias_ref, o_ref):
    j = pl.program_id(0)
    lo = _low_half((TM, LANES))
    n_loc = 3 * TM

    def stacked_queries(pp):
        q = q_ref[:, pp * LANES:(pp + 1) * LANES]
        zero = jnp.zeros_like(q)
        return jnp.concatenate([jnp.where(lo, q, zero), jnp.where(lo, zero, q)], axis=0)

    def store(pp, ot):
        o_ref[:, pp * LANES:(pp + 1) * LANES] = _pack_heads_t(ot[:, :TM], ot[:, TM:]).astype(o_ref.dtype)

    @pl.when(j < NT_LAT)
    def _():
        for pp in range(PAIRS_PER_STEP):
            cs = slice(pp * LANES, (pp + 1) * LANES)
            k_all = jnp.concatenate([r[:, cs] for r in (k0_ref, k1_ref, k2_ref, kc_ref)], axis=0)
            vt_all = jnp.concatenate([r[cs, :] for r in (v0_ref, v1_ref, v2_ref, vc_ref)], axis=1)
            s = _dot_nt(k_all, stacked_queries(pp))
            blocks = [s[i * TM:(i + 1) * TM] + bias_ref[pp, i * TM:(i + 1) * TM, :] for i in range(3)]
            store(pp, _attend_t(blocks + [s[n_loc:]], vt_all))

    @pl.when(j >= NT_LAT)
    def _():
        for pp in range(PAIRS_PER_STEP):
            cs = slice(pp * LANES, (pp + 1) * LANES)
            store(pp, _attend_t([_dot_nt(kc_ref[:, cs], stacked_queries(pp))], vc_ref[cs, :]))


def _na_attention(qa, ka, vat, bias):
    def kstart(j):
        return jnp.clip(j - 1, 0, NT_LAT - 3)

    def bias_class(j):
        return jnp.where(j == 0, 0, jnp.where(j == NT_LAT - 1, 2, 1))

    pps = PAIRS_PER_STEP
    blk = (None, TM, pps * LANES)
    blk_t = (None, pps * LANES, TM)
    q_spec = pl.BlockSpec(blk, lambda j, hp, b: (b, j, hp))
    k_specs = [pl.BlockSpec(blk, functools.partial(lambda i, j, hp, b: (b, kstart(j) + i, hp), i))
               for i in range(3)]
    k_specs.append(pl.BlockSpec(blk, lambda j, hp, b: (b, NT_LAT, hp)))
    v_specs = [pl.BlockSpec(blk_t, functools.partial(lambda i, j, hp, b: (b, hp, kstart(j) + i), i))
               for i in range(3)]
    v_specs.append(pl.BlockSpec(blk_t, lambda j, hp, b: (b, hp, NT_LAT)))
    bias_spec = pl.BlockSpec((None, pps, 3 * TM, 2 * TM), lambda j, hp, b: (bias_class(j), hp, 0, 0))
    return pl.pallas_call(
        _na_kernel,
        grid=(NT_ALL, NA_HEADS // (2 * pps), BATCH),
        in_specs=[q_spec] + k_specs + v_specs + [bias_spec],
        out_specs=q_spec,
        out_shape=jax.ShapeDtypeStruct((BATCH, T_ALL, NA_WIDTH), BF16),
        compiler_params=_params(("arbitrary", "arbitrary", "arbitrary")),
        name="na_attention",
    )(qa, ka, ka, ka, ka, vat, vat, vat, vat, bias)


def _na_bias_tables():
    rows = SEQ // GRID_W
    n_dr, n_dc = 2 * NA_WIN_R - 1, 2 * NA_WIN_C - 1
    kc = np.arange(GRID_W)
    qc = np.arange(GRID_W)
    w_start = np.clip(qc - NA_WIN_C // 2, 0, GRID_W - NA_WIN_C)
    col_ok = (kc[None, :] >= w_start[:, None]) & (kc[None, :] < w_start[:, None] + NA_WIN_C)
    dcol = np.clip(kc[None, :] - qc[:, None] + NA_WIN_C - 1, 0, n_dc - 1)
    col_sel = (dcol[None] == np.arange(n_dc)[:, None, None]).astype(np.float32)
    row_ok, row_sel = [], []
    for j in (0, 1, NT_LAT - 1):
        r = NA_QROWS * j + np.arange(NA_QROWS)
        r0 = np.clip(r - NA_WIN_R // 2, 0, rows - NA_WIN_R)
        start = NA_QROWS * int(np.clip(j - 1, 0, NT_LAT - 3))
        krow = start + np.arange(NA_KROWS)
        row_ok.append((krow[None, :] >= r0[:, None]) & (krow[None, :] < r0[:, None] + NA_WIN_R))
        drow = np.clip(krow[None, :] - r[:, None] + NA_WIN_R - 1, 0, n_dr - 1)
        row_sel.append((drow[..., None] == np.arange(n_dr)).astype(np.float32))
    return col_sel, col_ok, np.stack(row_sel), np.stack(row_ok)


def _na_bias(rpb):
    col_sel, col_ok, row_sel, row_ok = _na_bias_tables()
    hi = lax.Precision.HIGHEST
    by_col = jnp.einsum('hrd,dck->hrkc', rpb, col_sel, precision=hi)
    by_col = by_col.reshape(NA_HEADS // 2, 2, *by_col.shape[1:])
    full = jnp.einsum('samr,perkc->spmkeac', row_sel, by_col, precision=hi)
    ok = (row_ok.transpose(0, 2, 1)[:, None, :, None, None, :, None]
          & col_ok.T[None, None, None, :, None, None, :])
    full = jnp.where(ok, full * LOG2E, NEG)
    return full.reshape(3, NA_HEADS // 2, NA_KROWS * GRID_W, 2 * NA_QROWS * GRID_W)


MLA_KCHUNK = 384


def _mla_kernel(q_ref, k_ref, vt_ref, o_ref):
    jq = pl.program_id(2)
    first = lax.broadcasted_iota(jnp.int32, (TM, 2 * LANES), 1) < LANES

    def run(k_lo, k_hi):
        for pp in range(PAIRS_PER_STEP):
            cs = slice(pp * 2 * LANES, (pp + 1) * 2 * LANES)
            q = q_ref[:, cs]
            zero = jnp.zeros_like(q)
            qs = jnp.concatenate([jnp.where(first, q, zero), jnp.where(first, zero, q)], axis=0)
            s = _dot_nt(k_ref[k_lo:k_hi, cs], qs)
            n = k_hi - k_lo
            chunks = [s[c0:min(c0 + MLA_KCHUNK, n)] for c0 in range(0, n, MLA_KCHUNK)]
            ot = _attend_t(chunks, vt_ref[pp * LANES:(pp + 1) * LANES, k_lo:k_hi])
            o_ref[:, pp * LANES:(pp + 1) * LANES] = _pack_heads_t(ot[:, :TM], ot[:, TM:]).astype(o_ref.dtype)

    pl.when(jq < NT_LAT)(lambda: run(0, T_ALL))
    pl.when(jq >= NT_LAT)(lambda: run(SEQ, T_ALL))


def _mla_attention(qm, km, vmt):
    pps = PAIRS_PER_STEP
    return pl.pallas_call(
        _mla_kernel,
        grid=(BATCH, MLA_HEADS // (2 * pps), NT_ALL),
        in_specs=[pl.BlockSpec((None, TM, pps * 2 * LANES), lambda b, hp, j: (b, j, hp)),
                  pl.BlockSpec((None, T_ALL, pps * 2 * LANES), lambda b, hp, j: (b, 0, hp)),
                  pl.BlockSpec((None, pps * LANES, T_ALL), lambda b, hp, j: (b, hp, 0))],
        out_specs=pl.BlockSpec((None, TM, pps * LANES), lambda b, hp, j: (b, j, hp)),
        out_shape=jax.ShapeDtypeStruct((BATCH, T_ALL, MLA_HEADS * MLA_V), BF16),
        compiler_params=_params(("arbitrary", "arbitrary", "arbitrary")),
        name="mla_attention",
    )(qm, km, vmt)


def _inproj_odd_kernel(x_ref, sh_ref, sc_ref, w_in_ref, cq_ref, sq_ref, ck_ref, sk_ref, q_ref, k_ref, v_ref):
    h = (x_ref[...] * (1.0 + sc_ref[...]) + sh_ref[...]).astype(BF16)
    z = _dot(h, w_in_ref[...])
    nq = SWA_Q_HEADS * HEAD_DIM
    nk = 2 * SWA_KV_HEADS * HEAD_DIM
    cq_t, sq_t, ck_t, sk_t = cq_ref[...], sq_ref[...], ck_ref[...], sk_ref[...]
    for g in range(nq // LANES):
        sl = slice(g * LANES, (g + 1) * LANES)
        q_ref[:, sl] = _rope(z[:, sl], cq_t, sq_t, HEAD_DIM // 4).astype(BF16)
    for g in range(nk // LANES):
        sl = slice(g * LANES, (g + 1) * LANES)
        k_ref[:, sl] = _rope(z[:, nq + g * LANES:nq + (g + 1) * LANES], ck_t, sk_t, HEAD_DIM // 4).astype(BF16)
    v_ref[...] = z[:, nq + nk:].T.astype(BF16)


def _inproj_odd(x, sh, sc, w_in, tabs):
    cq, sq, ck, sk = tabs
    tab = pl.BlockSpec((TM, LANES), lambda b, t: (t, 0))
    nq = SWA_Q_HEADS * HEAD_DIM
    nk = 2 * SWA_KV_HEADS * HEAD_DIM
    nv = SWA_KV_HEADS * HEAD_DIM
    return pl.pallas_call(
        _inproj_odd_kernel,
        grid=(BATCH, NT_ALL),
        in_specs=[_tok_spec(D_MODEL), _mod_spec(), _mod_spec(), _const_spec(w_in.shape), tab, tab, tab, tab],
        out_specs=[_tok_spec(nq), _tok_spec(nk), _tok_spec_t(nv)],
        out_shape=[jax.ShapeDtypeStruct((BATCH, T_ALL, nq), BF16), jax.ShapeDtypeStruct((BATCH, T_ALL, nk), BF16),
                   jax.ShapeDtypeStruct((BATCH, nv, T_ALL), BF16)],
        compiler_params=_params(("arbitrary", "arbitrary")),
        name="inproj_odd",
    )(x, sh, sc, w_in, cq, sq, ck, sk)


SWA_GROUP = SWA_Q_HEADS // SWA_KV_HEADS
N_QBLK_LAT = SEQ // SWA_BLOCK


def _swa_kernel(q_ref, k0_ref, k1_ref, k2_ref, kc_ref, v0_ref, v1_ref, v2_ref, vc_ref, band_ref, sink_ref, o_ref):
    n = pl.program_id(1)
    lo = _low_half((SWA_BLOCK, LANES))
    is_ctx = n >= N_QBLK_LAT
    pen0 = jnp.where((n == 0) | is_ctx, NEG, 0.0)
    pen1 = jnp.where(is_ctx, NEG, 0.0)
    pen2 = jnp.where(n >= N_QBLK_LAT - 1, NEG, 0.0)
    mask0 = band_ref[0] + pen0
    mask2 = band_ref[1] + pen2
    b = SWA_BLOCK
    gw = SWA_GROUP * HEAD_DIM
    for g in range(SWA_KV_HEADS):
        parts = []
        for pr in range(SWA_GROUP // 2):
            qp = q_ref[:, g * gw + pr * LANES:g * gw + (pr + 1) * LANES]
            zero = jnp.zeros_like(qp)
            parts.append(jnp.where(lo, qp, zero))
            parts.append(jnp.where(lo, zero, qp))
        qs = jnp.concatenate(parts, axis=0)
        ks = slice(g * LANES, (g + 1) * LANES)
        vs = slice(g * HEAD_DIM, (g + 1) * HEAD_DIM)
        k_all = jnp.concatenate([r[:, ks] for r in (k0_ref, k1_ref, k2_ref, kc_ref)], axis=0)
        vt_all = jnp.concatenate([r[vs, :] for r in (v0_ref, v1_ref, v2_ref, vc_ref)], axis=1)
        s = _dot_nt(k_all, qs)
        blocks = [s[:b] + mask0, s[b:2 * b] + pen1, s[2 * b:3 * b] + mask2, s[3 * b:]]
        ot = _attend_t(blocks, vt_all, sink=sink_ref[g])
        for pr in range(SWA_GROUP // 2):
            pair = jnp.concatenate([ot[:, (2 * pr) * b:(2 * pr + 1) * b], ot[:, (2 * pr + 1) * b:(2 * pr + 2) * b]],
                                   axis=0)
            o_ref[:, g * gw + pr * LANES:g * gw + (pr + 1) * LANES] = pair.T.astype(o_ref.dtype)


def _swa_band():
    kk = np.arange(SWA_BLOCK)[:, None]
    qi = np.arange(SWA_BLOCK)[None, :]
    before = np.where(kk >= qi, 0.0, NEG)
    after = np.where(kk <= qi, 0.0, NEG)
    band = np.stack([before, after]).astype(np.float32)
    return np.tile(band, (1, 1, SWA_GROUP))


def _swa_attention(q, kd, vt, sink_row, n_qblk):
    rows = SWA_GROUP * SWA_BLOCK
    kw = SWA_KV_HEADS * LANES
    vw = SWA_KV_HEADS * HEAD_DIM
    q_spec = pl.BlockSpec((None, SWA_BLOCK, SWA_Q_HEADS * HEAD_DIM), lambda b, n: (b, n, 0))

    def near(i):
        return lambda n: jnp.clip(n - 1 + i, 0, N_QBLK_LAT - 1)

    k_specs = [pl.BlockSpec((None, SWA_BLOCK, kw), functools.partial(lambda f, b, n: (b, f(n), 0), near(i)))
               for i in range(3)]
    k_specs.append(pl.BlockSpec((None, CTX_LEN, kw), lambda b, n: (b, SEQ // CTX_LEN, 0)))
    v_specs = [pl.BlockSpec((None, vw, SWA_BLOCK), functools.partial(lambda f, b, n: (b, 0, f(n)), near(i)))
               for i in range(3)]
    v_specs.append(pl.BlockSpec((None, vw, CTX_LEN), lambda b, n: (b, 0, SEQ // CTX_LEN)))
    band_spec = _const_spec((2, SWA_BLOCK, rows))
    sink_spec = _const_spec((SWA_KV_HEADS, 1, rows))
    return pl.pallas_call(
        _swa_kernel,
        grid=(BATCH, n_qblk),
        in_specs=[q_spec] + k_specs + v_specs + [band_spec, sink_spec],
        out_specs=q_spec,
        out_shape=jax.ShapeDtypeStruct((BATCH, n_qblk * SWA_BLOCK, SWA_Q_HEADS * HEAD_DIM), BF16),
        compiler_params=_params(("arbitrary", "arbitrary")),
        name="swa_attention",
    )(q, kd, kd, kd, kd, vt, vt, vt, vt, jnp.asarray(_swa_band()), sink_row)


def _mix_ffn_kernel(n_o, n_tiles, *refs):
    o_refs, w_refs = refs[:3 * n_o], refs[3 * n_o:4 * n_o]
    (x_ref, xp_ref, xn_ref, gm_ref, sh_ref, sc_ref, gf_ref, w_up_ref, b_up_ref, cw_ref, cb_ref,
     w_dn_ref, b_dn_ref, out_ref, act_ref) = refs[4 * n_o:]
    t = pl.program_id(1)
    y = None
    for i in range(n_o):
        tile, prev, nxt = o_refs[3 * i:3 * i + 3]
        d = _dot(jnp.concatenate([prev[...], tile[...], nxt[...]], axis=0), w_refs[i][...])
        y = d if y is None else y + d
    x0 = jnp.concatenate([xp_ref[...], x_ref[...], xn_ref[...]], axis=0)
    x1 = _layer_norm(DEEPNORM_ALPHA * x0 + gm_ref[...] * y)
    x = x1[NBR:NBR + TM]
    rows = TM + 2 * HALO
    h = (x1[NBR - HALO:NBR + TM + HALO] * (1.0 + sc_ref[...]) + sh_ref[...]).astype(BF16)
    prev_ok = (t != 0) & (t != NT_LAT)
    next_ok = (t != NT_LAT - 1) & (t != n_tiles - 1)
    for c in range(N_FF_CHUNKS):
        b_up = b_up_ref[c]
        cw = cw_ref[c]
        w0, w1, w2 = cw[0:1], cw[1:2], cw[2:3]
        u = _dot(h, w_up_ref[c])
        u = jnp.concatenate([jnp.where(prev_ok, u[:HALO], -b_up), u[HALO:HALO + TM],
                             jnp.where(next_ok, u[HALO + TM:], -b_up)], axis=0)
        cb = cb_ref[c] + b_up * (w0 + w1 + w2)
        v = (w0 * pltpu.roll(u, 1, 0) + w1 * u + w2 * pltpu.roll(u, rows - 1, 0) + cb)[HALO:HALO + TM]
        act_ref[:, c * FF_CHUNK:(c + 1) * FF_CHUNK] = (v[:, :FF_CHUNK] * _silu(v[:, FF_CHUNK:])).astype(BF16)
    y2 = _dot(act_ref[...], w_dn_ref[...]) + b_dn_ref[...]
    out_ref[...] = _layer_norm(DEEPNORM_ALPHA * x + gf_ref[...] * y2)


def _mix_ffn(o_list, w_list, x, gm, sh, sc, gf, w_up, b_up, cw, cb, w_dn, b_dn, n_tiles):
    per_tile = TM // NBR
    last = n_tiles * per_tile - 1

    def with_neighbours(width):
        return [_tok_spec(width),
                pl.BlockSpec((None, NBR, width), lambda b, t: (b, jnp.maximum(t * per_tile - 1, 0), 0)),
                pl.BlockSpec((None, NBR, width), lambda b, t: (b, jnp.minimum((t + 1) * per_tile, last), 0))]

    o_specs, o_args = [], []
    for o in o_list:
        o_specs += with_neighbours(o.shape[-1])
        o_args += [o, o, o]
    consts = [w_up, b_up, cw, cb, w_dn, b_dn]
    return pl.pallas_call(
        functools.partial(_mix_ffn_kernel, len(o_list), n_tiles),
        grid=(BATCH, n_tiles),
        in_specs=o_specs + [_const_spec(w.shape) for w in w_list] + with_neighbours(D_MODEL)
                 + [_mod_spec()] * 4 + [_const_spec(a.shape) for a in consts],
        out_specs=_tok_spec(D_MODEL),
        out_shape=jax.ShapeDtypeStruct((BATCH, n_tiles * TM, D_MODEL), F32),
        scratch_shapes=[pltpu.VMEM((TM, D_FF), BF16)],
        compiler_params=_params(("arbitrary", "arbitrary")),
        name="mix_ffn_ln",
    )(*o_args, *w_list, x, x, x, gm, sh, sc, gf, *consts)


def _rope_tables(rot_dim):
    axis_dim = rot_dim // 2
    t = jnp.arange(SEQ)
    row = (t // GRID_W).astype(F32)[:, None]
    col = (t % GRID_W).astype(F32)[:, None]
    inv_freq = ROPE_BASE ** (-jnp.arange(0, axis_dim, 2, dtype=F32) / axis_dim)
    ar, ac = row * inv_freq, col * inv_freq
    ang = jnp.concatenate([ar, ar, ac, ac], axis=-1)
    sign = np.where(np.arange(rot_dim) % axis_dim < axis_dim // 2, -1.0, 1.0).astype(np.float32)
    return jnp.cos(ang), jnp.sin(ang) * sign


def _with_ctx_rows(cos, sin):
    ones = jnp.ones((CTX_LEN, cos.shape[1]), F32)
    return jnp.concatenate([cos, ones], axis=0), jnp.concatenate([sin, jnp.zeros_like(ones)], axis=0)


def _mla_tables():
    cos, sin = _rope_tables(MLA_ROPE)
    pad_l = jnp.ones((SEQ, MLA_NOPE), F32)
    pad_r = jnp.ones((SEQ, LANES - MLA_NOPE - MLA_ROPE), F32)
    cos = jnp.concatenate([pad_l, cos, pad_r], axis=-1)
    sin = jnp.concatenate([0 * pad_l, sin, 0 * pad_r], axis=-1)
    cos, sin = _with_ctx_rows(cos, sin)
    qs = (MLA_NOPE + MLA_ROPE) ** -0.5 * LOG2E
    return cos * qs, sin * qs, cos, sin


def _swa_tables():
    cos, sin = _rope_tables(HEAD_DIM)
    cos, sin = _with_ctx_rows(jnp.tile(cos, (1, 2)), jnp.tile(sin, (1, 2)))
    qs = HEAD_DIM ** -0.5 * LOG2E
    return cos * qs, sin * qs, cos, sin


def _even_weights(w_in, w_uq, w_ukv, w_out):
    o3 = 3 * NA_WIDTH + MLA_Q_RANK + MLA_KV_RANK
    pad_l = jnp.zeros((D_MODEL, MLA_NOPE), F32)
    pad_r = jnp.zeros((D_MODEL, LANES - MLA_NOPE - MLA_ROPE), F32)
    w_in_p = jnp.concatenate([w_in[:, :o3], pad_l, w_in[:, o3:], pad_r], axis=-1).astype(BF16)
    uq = w_uq.reshape(MLA_Q_RANK, MLA_HEADS, MLA_NOPE + MLA_ROPE)
    uq = jnp.pad(uq, ((0, 0), (0, 0), (0, LANES - MLA_NOPE - MLA_ROPE))).reshape(MLA_Q_RANK, MLA_HEADS * LANES)
    ukv = w_ukv.reshape(MLA_KV_RANK, MLA_HEADS, MLA_NOPE + MLA_V)
    ukk = jnp.pad(ukv[:, :, :MLA_NOPE], ((0, 0), (0, 0), (0, LANES - MLA_NOPE))).reshape(MLA_KV_RANK, MLA_HEADS * LANES)
    ukvv = ukv[:, :, MLA_NOPE:].reshape(MLA_KV_RANK, MLA_HEADS * MLA_V)
    wo = w_out.astype(BF16)
    return w_in_p, uq.astype(BF16), ukk.astype(BF16), ukvv.astype(BF16), wo[:NA_WIDTH], wo[NA_WIDTH:]


def _odd_weights(w_in, w_out):
    nq = SWA_Q_HEADS * HEAD_DIM
    nkv = SWA_KV_HEADS * HEAD_DIM
    dup = lambda w: jnp.repeat(w.reshape(D_MODEL, SWA_KV_HEADS, 1, HEAD_DIM), 2, axis=2).reshape(D_MODEL, 2 * nkv)
    w_in_p = jnp.concatenate([w_in[:, :nq], dup(w_in[:, nq:nq + nkv]), w_in[:, nq + nkv:]], axis=-1)
    return w_in_p.astype(BF16), w_out.astype(BF16)


def _ffn_weights(w_up, b_up, conv_w, conv_b, w_down, b_down):
    def chunked(a):
        lead = a.shape[:-1]
        a = a.reshape(*lead, 2, N_FF_CHUNKS, FF_CHUNK)
        a = jnp.moveaxis(a, -2, 0)
        return a.reshape(N_FF_CHUNKS, *lead, 2 * FF_CHUNK)
    return (chunked(w_up).astype(BF16), chunked(b_up[None]), chunked(conv_w), chunked(conv_b[None]),
            w_down.astype(BF16), b_down[None])


def kernel(x, c, ctx, c_ctx, w_ada, b_ada, na_rpb, w_in_even, mla_q_norm, w_uq, mla_kv_norm, w_ukv, w_out_even,
           w_in_odd, sinks, w_out_odd, w_up, b_up, conv_w, conv_b, w_down, b_down):
    cs = jnp.concatenate([c, c_ctx[None], jnp.zeros((7, D_MODEL), F32)], axis=0)
    mod = _modulation(cs, w_ada, b_ada)
    mod_lat = mod[:, :BATCH].reshape(DEPTH, BATCH, 1, 6, D_MODEL)
    mod_ctx = jnp.broadcast_to(mod[:, BATCH].reshape(DEPTH, 1, 1, 6, D_MODEL), mod_lat.shape)
    mods = jnp.concatenate([mod_lat, mod_ctx], axis=2).reshape(DEPTH, 2 * BATCH, 6, 1, D_MODEL)

    mla_tabs = _mla_tables()
    swa_tabs = _swa_tables()
    xs = jnp.concatenate([x, ctx], axis=1)
    for l in range(DEPTH):
        i = l // 2
        last = l == DEPTH - 1
        n_tiles = NT_LAT if last else NT_ALL
        sh_m, sc_m, g_m, sh_f, sc_f, g_f = (mods[l, :, k] for k in range(6))
        if l % 2 == 0:
            w_in_p, uq, ukk, ukvv, wo_a, wo_b = _even_weights(w_in_even[i], w_uq[i], w_ukv[i], w_out_even[i])
            qa, ka, vat, qm, km, vmt = _inproj_even(xs, sh_m, sc_m, w_in_p, mla_q_norm[i][None], uq,
                                                    mla_kv_norm[i][None], ukk, ukvv, mla_tabs)
            o_a = _na_attention(qa, ka, vat, _na_bias(na_rpb[i]))
            o_b = _mla_attention(qm, km, vmt)
            o_list, wo_list = [o_a, o_b], [wo_a, wo_b]
        else:
            w_in_p, wo = _odd_weights(w_in_odd[i], w_out_odd[i])
            q, kd, vt = _inproj_odd(xs, sh_m, sc_m, w_in_p, swa_tabs)
            sink_row = jnp.repeat(sinks[i] * LOG2E, SWA_BLOCK).reshape(SWA_KV_HEADS, 1, SWA_GROUP * SWA_BLOCK)
            n_qblk = N_QBLK_LAT if last else T_ALL // SWA_BLOCK
            o = _swa_attention(q, kd, vt, sink_row, n_qblk)
            o_list, wo_list = [o], [wo]
        fw = _ffn_weights(w_up[l], b_up[l], conv_w[l], conv_b[l], w_down[l], b_down[l])
        xs = _mix_ffn(o_list, wo_list, xs, g_m, sh_f, sc_f, g_f, *fw, n_tiles)
    return xs
```

```python
import functools

import numpy as np
import jax
import jax.numpy as jnp
from jax import lax
from jax.experimental import pallas as pl
from jax.experimental.pallas import tpu as pltpu

D_MODEL = 1024
BATCH = 16
SEQ = 2048
CTX_LEN = 256
T_ALL = SEQ + CTX_LEN
DEPTH = 4
GRID_W = 64
HEAD_DIM = 64
NA_HEADS = 8
NA_WIN_R = 8
NA_WIN_C = 16
MLA_HEADS = 8
MLA_Q_RANK = 384
MLA_KV_RANK = 256
MLA_NOPE = 64
MLA_ROPE = 32
MLA_V = 64
SWA_Q_HEADS = 16
SWA_KV_HEADS = 2
SWA_WINDOW = 128
SWA_BLOCK = 128
D_FF = 2816
CONV_W = 3
ROPE_BASE = 10000.0
LN_EPS = 1e-6
RMS_EPS = 1e-6
NEG = -1e30
DEEPNORM_ALPHA = (2 * DEPTH) ** 0.25
NA_WIDTH = NA_HEADS * HEAD_DIM

LANES = 128
TM = 256
NT_LAT = SEQ // TM
NT_ALL = T_ALL // TM
HALO = 8
NBR = 16
ROW_GROUP = 128
SUM_ROWS = 16
FF_CHUNK = 256
N_FF_CHUNKS = D_FF // FF_CHUNK
NA_QROWS = TM // GRID_W
NA_KROWS = 3 * NA_QROWS
VMEM_LIMIT = 56 * 1024 * 1024
LOG2E = 1.4426950408889634
PAIRS_PER_STEP = 4

F32 = jnp.float32
BF16 = jnp.bfloat16


def _params(sem):
    return pltpu.CompilerParams(dimension_semantics=sem, vmem_limit_bytes=VMEM_LIMIT)


def _dot(a, b):
    return jnp.dot(a, b, preferred_element_type=F32)


def _dot_nt(a, b):
    return lax.dot_general(a, b, (((1,), (1,)), ((), ())), preferred_element_type=F32)


def _layer_norm(x):
    mu = jnp.mean(x, axis=-1, keepdims=True)
    xc = x - mu
    var = jnp.mean(xc * xc, axis=-1, keepdims=True)
    return xc * lax.rsqrt(var + LN_EPS)


def _rms_norm(x, g):
    return x * lax.rsqrt(jnp.mean(x * x, axis=-1, keepdims=True) + RMS_EPS) * g


def _silu(x):
    return x / (1.0 + jnp.exp(-x))


def _rope(x, cos, sin, chunk):
    lane = lax.broadcasted_iota(jnp.int32, x.shape, 1)
    first = (lane % (2 * chunk)) < chunk
    rot = jnp.where(first, pltpu.roll(x, LANES - chunk, 1), pltpu.roll(x, chunk, 1))
    return x * cos + rot * sin


def _attend_t(s_list, vt, sink=None, vpu_bound=True):
    dv, n_keys = vt.shape
    if vpu_bound:
        vt = jnp.concatenate([vt, jnp.ones((SUM_ROWS, n_keys), BF16)], axis=0)
    m = l = o = None
    k0 = 0
    for s in s_list:
        k1 = k0 + s.shape[0]
        m_blk = jnp.max(s, axis=0, keepdims=True)
        m_new = m_blk if m is None else jnp.maximum(m, m_blk)
        if vpu_bound:
            pv_sum = _dot(vt[:, k0:k1], jnp.exp2((s - m_new).astype(BF16)))
            pv, p_sum = pv_sum[:dv], pv_sum[dv:dv + 1]
        else:
            p = jnp.exp2(s - m_new)
            p_sum = jnp.sum(p, axis=0, keepdims=True)
            pv = _dot(vt[:, k0:k1], p.astype(BF16))
        if m is None:
            l, o = p_sum, pv
        else:
            alpha = jnp.exp2(m - m_new)
            l = l * alpha + p_sum
            o = o * alpha + pv
        m, k0 = m_new, k1
    if sink is not None:
        m_new = jnp.maximum(m, sink)
        alpha = jnp.exp2(m - m_new)
        l = l * alpha + jnp.exp2(sink - m_new)
        o = o * alpha
    return o * (1.0 / l)


def _low_half(shape):
    return lax.broadcasted_iota(jnp.int32, shape, 1) < HEAD_DIM


def _pack_heads_t(ot0, ot1):
    top = lax.broadcasted_iota(jnp.int32, ot0.shape, 0) < HEAD_DIM
    return jnp.where(top, ot0, ot1).T


def _mod_kernel(c_ref, w_ref, b_ref, o_ref):
    a = _silu(c_ref[...]).astype(BF16)
    o_ref[...] = _dot(a, w_ref[...].astype(BF16)) + b_ref[...]


def _modulation(cs, w_ada, b_ada):
    rows = cs.shape[0]
    tn = 1536
    return pl.pallas_call(
        _mod_kernel,
        grid=(DEPTH, 6 * D_MODEL // tn),
        in_specs=[
            pl.BlockSpec((rows, D_MODEL), lambda l, n: (0, 0)),
            pl.BlockSpec((None, D_MODEL, tn), lambda l, n: (l, 0, n)),
            pl.BlockSpec((None, 1, tn), lambda l, n: (l, 0, n)),
        ],
        out_specs=pl.BlockSpec((None, rows, tn), lambda l, n: (l, 0, n)),
        out_shape=jax.ShapeDtypeStruct((DEPTH, rows, 6 * D_MODEL), F32),
        compiler_params=_params(("arbitrary", "arbitrary")),
        name="adaln_mod",
    )(cs, w_ada, b_ada.reshape(DEPTH, 1, 6 * D_MODEL))


def _mod_spec():
    return pl.BlockSpec((None, 1, D_MODEL), lambda b, t: (2 * b + jnp.where(t >= NT_LAT, 1, 0), 0, 0))


def _const_spec(shape):
    return pl.BlockSpec(shape, lambda *_: (0,) * len(shape))


def _tok_spec(width):
    return pl.BlockSpec((None, TM, width), lambda b, t: (b, t, 0))


def _tok_spec_t(width):
    return pl.BlockSpec((None, width, TM), lambda b, t: (b, 0, t))


def _stream_specs(split):
    if not split:
        return [_tok_spec(D_MODEL)]
    return [pl.BlockSpec((None, TM, D_MODEL), lambda b, t: (b, jnp.minimum(t, NT_LAT - 1), 0)),
            pl.BlockSpec((None, CTX_LEN, D_MODEL), lambda b, t: (b, 0, 0))]


def _stream_rows(x_refs, rs):
    if len(x_refs) == 1:
        return x_refs[0][rs, :]
    return jnp.where(pl.program_id(1) >= NT_LAT, x_refs[1][rs, :], x_refs[0][rs, :])


def _inproj_even_kernel(n_x, *refs):
    x_refs = refs[:n_x]
    (sh_ref, sc_ref, w_in_ref, qg_ref, w_uq_ref, kg_ref, w_ukk_ref, w_ukv_ref, cq_ref, sq_ref, ck_ref, sk_ref,
     qa_ref, ka_ref, va_ref, qm_ref, km_ref, vm_ref) = refs[n_x:]
    w = NA_WIDTH
    o1 = 3 * w
    o2 = o1 + MLA_Q_RANK
    o3 = o2 + MLA_KV_RANK
    scale = 1.0 + sc_ref[...]
    shift = sh_ref[...]
    for rg in range(TM // ROW_GROUP):
        rs = slice(rg * ROW_GROUP, (rg + 1) * ROW_GROUP)
        h = (_stream_rows(x_refs, rs) * scale + shift).astype(BF16)
        z = _dot(h, w_in_ref[...])
        qa_ref[rs, :] = (z[:, :w] * (HEAD_DIM ** -0.5 * LOG2E)).astype(BF16)
        ka_ref[rs, :] = z[:, w:2 * w].astype(BF16)
        va_ref[:, rs] = z[:, 2 * w:3 * w].T.astype(BF16)
        cq = _rms_norm(z[:, o1:o2], qg_ref[...]).astype(BF16)
        ckv = _rms_norm(z[:, o2:o3], kg_ref[...]).astype(BF16)
        q = _dot(cq, w_uq_ref[...])
        kn = _dot(ckv, w_ukk_ref[...])
        vm_ref[:, rs] = _dot(ckv, w_ukv_ref[...]).T.astype(BF16)
        kr = _rope(z[:, o3:o3 + LANES], ck_ref[rs, :], sk_ref[rs, :], MLA_ROPE // 4)
        cq_t = cq_ref[rs, :]
        sq_t = sq_ref[rs, :]
        for hh in range(MLA_HEADS):
            sl = slice(hh * LANES, (hh + 1) * LANES)
            qm_ref[rs, sl] = _rope(q[:, sl], cq_t, sq_t, MLA_ROPE // 4).astype(BF16)
            km_ref[rs, sl] = (kn[:, sl] + kr).astype(BF16)


def _inproj_even(xs, sh, sc, w_in, qg, w_uq, kg, w_ukk, w_ukv, tabs):
    cq, sq, ck, sk = tabs
    tab = pl.BlockSpec((TM, LANES), lambda b, t: (t, 0))
    n_in = w_in.shape[1]
    outs = [(NA_WIDTH, False), (NA_WIDTH, False), (NA_WIDTH, True),
            (MLA_HEADS * LANES, False), (MLA_HEADS * LANES, False), (MLA_HEADS * MLA_V, True)]
    return pl.pallas_call(
        functools.partial(_inproj_even_kernel, len(xs)),
        grid=(BATCH, NT_ALL),
        in_specs=_stream_specs(len(xs) > 1) + [_mod_spec(), _mod_spec(),
                  _const_spec((D_MODEL, n_in)),
                  _const_spec((1, MLA_Q_RANK)), _const_spec(w_uq.shape),
                  _const_spec((1, MLA_KV_RANK)), _const_spec(w_ukk.shape), _const_spec(w_ukv.shape),
                  tab, tab, tab, tab],
        out_specs=[_tok_spec_t(wd) if tr else _tok_spec(wd) for wd, tr in outs],
        out_shape=[jax.ShapeDtypeStruct((BATCH, wd, T_ALL) if tr else (BATCH, T_ALL, wd), BF16) for wd, tr in outs],
        compiler_params=_params(("arbitrary", "arbitrary")),
        name="inproj_even",
    )(*xs, sh, sc, w_in, qg, w_uq, kg, w_ukk, w_ukv, cq, sq, ck, sk)


def _na_kernel(q_ref, k0_ref, k1_ref, k2_ref, kc_ref, v0_ref, v1_ref, v2_ref, vc_ref, bias_ref, o_ref):
    j = pl.program_id(0)
    lo = _low_half((TM, LANES))
    n_loc = 3 * TM

    def stacked_queries(pp):
        q = q_ref[:, pp * LANES:(pp + 1) * LANES]
        zero = jnp.zeros_like(q)
        return jnp.concatenate([jnp.where(lo, q, zero), jnp.where(lo, zero, q)], axis=0)

    def store(pp, ot):
        o_ref[:, pp * LANES:(pp + 1) * LANES] = _pack_heads_t(ot[:, :TM], ot[:, TM:]).astype(o_ref.dtype)

    @pl.when(j < NT_LAT)
    def _():
        for pp in range(PAIRS_PER_STEP):
            cs = slice(pp * LANES, (pp + 1) * LANES)
            k_all = jnp.concatenate([r[:, cs] for r in (k0_ref, k1_ref, k2_ref, kc_ref)], axis=0)
            vt_all = jnp.concatenate([r[cs, :] for r in (v0_ref, v1_ref, v2_ref, vc_ref)], axis=1)
            s = _dot_nt(k_all, stacked_queries(pp))
            blocks = [s[i * TM:(i + 1) * TM] + bias_ref[pp, i * TM:(i + 1) * TM, :] for i in range(3)]
            store(pp, _attend_t(blocks + [s[n_loc:]], vt_all))

    @pl.when(j >= NT_LAT)
    def _():
        for pp in range(PAIRS_PER_STEP):
            cs = slice(pp * LANES, (pp + 1) * LANES)
            store(pp, _attend_t([_dot_nt(kc_ref[:, cs], stacked_queries(pp))], vc_ref[cs, :]))


def _na_attention(qa, ka, vat, bias):
    def kstart(j):
        return jnp.clip(j - 1, 0, NT_LAT - 3)

    def bias_class(j):
        return jnp.where(j == 0, 0, jnp.where(j == NT_LAT - 1, 2, 1))

    pps = PAIRS_PER_STEP
    blk = (None, TM, pps * LANES)
    blk_t = (None, pps * LANES, TM)
    q_spec = pl.BlockSpec(blk, lambda j, hp, b: (b, j, hp))
    k_specs = [pl.BlockSpec(blk, functools.partial(lambda i, j, hp, b: (b, kstart(j) + i, hp), i))
               for i in range(3)]
    k_specs.append(pl.BlockSpec(blk, lambda j, hp, b: (b, NT_LAT, hp)))
    v_specs = [pl.BlockSpec(blk_t, functools.partial(lambda i, j, hp, b: (b, hp, kstart(j) + i), i))
               for i in range(3)]
    v_specs.append(pl.BlockSpec(blk_t, lambda j, hp, b: (b, hp, NT_LAT)))
    bias_spec = pl.BlockSpec((None, pps, 3 * TM, 2 * TM), lambda j, hp, b: (bias_class(j), hp, 0, 0))
    return pl.pallas_call(
        _na_kernel,
        grid=(NT_ALL, NA_HEADS // (2 * pps), BATCH),
        in_specs=[q_spec] + k_specs + v_specs + [bias_spec],
        out_specs=q_spec,
        out_shape=jax.ShapeDtypeStruct((BATCH, T_ALL, NA_WIDTH), BF16),
        compiler_params=_params(("arbitrary", "arbitrary", "arbitrary")),
        name="na_attention",
    )(qa, ka, ka, ka, ka, vat, vat, vat, vat, bias)


def _na_bias_tables():
    rows = SEQ // GRID_W
    n_dr, n_dc = 2 * NA_WIN_R - 1, 2 * NA_WIN_C - 1
    kc = np.arange(GRID_W)
    qc = np.arange(GRID_W)
    w_start = np.clip(qc - NA_WIN_C // 2, 0, GRID_W - NA_WIN_C)
    col_ok = (kc[None, :] >= w_start[:, None]) & (kc[None, :] < w_start[:, None] + NA_WIN_C)
    dcol = np.clip(kc[None, :] - qc[:, None] + NA_WIN_C - 1, 0, n_dc - 1)
    col_sel = (dcol[None] == np.arange(n_dc)[:, None, None]).astype(np.float32)
    row_ok, row_sel = [], []
    for j in (0, 1, NT_LAT - 1):
        r = NA_QROWS * j + np.arange(NA_QROWS)
        r0 = np.clip(r - NA_WIN_R // 2, 0, rows - NA_WIN_R)
        start = NA_QROWS * int(np.clip(j - 1, 0, NT_LAT - 3))
        krow = start + np.arange(NA_KROWS)
        row_ok.append((krow[None, :] >= r0[:, None]) & (krow[None, :] < r0[:, None] + NA_WIN_R))
        drow = np.clip(krow[None, :] - r[:, None] + NA_WIN_R - 1, 0, n_dr - 1)
        row_sel.append((drow[..., None] == np.arange(n_dr)).astype(np.float32))
    return col_sel, col_ok, np.stack(row_sel), np.stack(row_ok)


def _na_bias(rpb):
    col_sel, col_ok, row_sel, row_ok = _na_bias_tables()
    hi = lax.Precision.HIGHEST
    by_col = jnp.einsum('hrd,dck->hrkc', rpb, col_sel, precision=hi)
    by_col = by_col.reshape(NA_HEADS // 2, 2, *by_col.shape[1:])
    full = jnp.einsum('samr,perkc->spmkeac', row_sel, by_col, precision=hi)
    ok = (row_ok.transpose(0, 2, 1)[:, None, :, None, None, :, None]
          & col_ok.T[None, None, None, :, None, None, :])
    full = jnp.where(ok, full * LOG2E, NEG)
    return full.reshape(3, NA_HEADS // 2, NA_KROWS * GRID_W, 2 * NA_QROWS * GRID_W)


MLA_KCHUNK = 384


def _mla_kernel(q_ref, k_ref, vt_ref, o_ref):
    jq = pl.program_id(2)
    first = lax.broadcasted_iota(jnp.int32, (TM, 2 * LANES), 1) < LANES

    def run(k_lo, k_hi):
        for pp in range(PAIRS_PER_STEP):
            cs = slice(pp * 2 * LANES, (pp + 1) * 2 * LANES)
            q = q_ref[:, cs]
            zero = jnp.zeros_like(q)
            qs = jnp.concatenate([jnp.where(first, q, zero), jnp.where(first, zero, q)], axis=0)
            s = _dot_nt(k_ref[k_lo:k_hi, cs], qs)
            n = k_hi - k_lo
            chunks = [s[c0:min(c0 + MLA_KCHUNK, n)] for c0 in range(0, n, MLA_KCHUNK)]
            ot = _attend_t(chunks, vt_ref[pp * LANES:(pp + 1) * LANES, k_lo:k_hi], vpu_bound=False)
            o_ref[:, pp * LANES:(pp + 1) * LANES] = _pack_heads_t(ot[:, :TM], ot[:, TM:]).astype(o_ref.dtype)

    pl.when(jq < NT_LAT)(lambda: run(0, T_ALL))
    pl.when(jq >= NT_LAT)(lambda: run(SEQ, T_ALL))


def _mla_attention(qm, km, vmt):
    pps = PAIRS_PER_STEP
    return pl.pallas_call(
        _mla_kernel,
        grid=(BATCH, MLA_HEADS // (2 * pps), NT_ALL),
        in_specs=[pl.BlockSpec((None, TM, pps * 2 * LANES), lambda b, hp, j: (b, j, hp)),
                  pl.BlockSpec((None, T_ALL, pps * 2 * LANES), lambda b, hp, j: (b, 0, hp)),
                  pl.BlockSpec((None, pps * LANES, T_ALL), lambda b, hp, j: (b, hp, 0))],
        out_specs=pl.BlockSpec((None, TM, pps * LANES), lambda b, hp, j: (b, j, hp)),
        out_shape=jax.ShapeDtypeStruct((BATCH, T_ALL, MLA_HEADS * MLA_V), BF16),
        compiler_params=_params(("arbitrary", "arbitrary", "arbitrary")),
        name="mla_attention",
    )(qm, km, vmt)


def _inproj_odd_kernel(x_ref, sh_ref, sc_ref, w_in_ref, cq_ref, sq_ref, ck_ref, sk_ref, q_ref, k_ref, v_ref):
    nq = SWA_Q_HEADS * HEAD_DIM
    nk = 2 * SWA_KV_HEADS * HEAD_DIM
    scale = 1.0 + sc_ref[...]
    shift = sh_ref[...]
    for rg in range(TM // ROW_GROUP):
        rs = slice(rg * ROW_GROUP, (rg + 1) * ROW_GROUP)
        h = (x_ref[rs, :] * scale + shift).astype(BF16)
        z = _dot(h, w_in_ref[...])
        cq_t, sq_t, ck_t, sk_t = cq_ref[rs, :], sq_ref[rs, :], ck_ref[rs, :], sk_ref[rs, :]
        for g in range(nq // LANES):
            sl = slice(g * LANES, (g + 1) * LANES)
            q_ref[rs, sl] = _rope(z[:, sl], cq_t, sq_t, HEAD_DIM // 4).astype(BF16)
        for g in range(nk // LANES):
            sl = slice(g * LANES, (g + 1) * LANES)
            k_ref[rs, sl] = _rope(z[:, nq + g * LANES:nq + (g + 1) * LANES], ck_t, sk_t,
                                  HEAD_DIM // 4).astype(BF16)
        v_ref[:, rs] = z[:, nq + nk:].T.astype(BF16)


def _inproj_odd(x, sh, sc, w_in, tabs):
    cq, sq, ck, sk = tabs
    tab = pl.BlockSpec((TM, LANES), lambda b, t: (t, 0))
    nq = SWA_Q_HEADS * HEAD_DIM
    nk = 2 * SWA_KV_HEADS * HEAD_DIM
    nv = SWA_KV_HEADS * HEAD_DIM
    return pl.pallas_call(
        _inproj_odd_kernel,
        grid=(BATCH, NT_ALL),
        in_specs=[_tok_spec(D_MODEL), _mod_spec(), _mod_spec(), _const_spec(w_in.shape), tab, tab, tab, tab],
        out_specs=[_tok_spec(nq), _tok_spec(nk), _tok_spec_t(nv)],
        out_shape=[jax.ShapeDtypeStruct((BATCH, T_ALL, nq), BF16), jax.ShapeDtypeStruct((BATCH, T_ALL, nk), BF16),
                   jax.ShapeDtypeStruct((BATCH, nv, T_ALL), BF16)],
        compiler_params=_params(("arbitrary", "arbitrary")),
        name="inproj_odd",
    )(x, sh, sc, w_in, cq, sq, ck, sk)


SWA_GROUP = SWA_Q_HEADS // SWA_KV_HEADS
N_QBLK_LAT = SEQ // SWA_BLOCK


def _swa_kernel(q_ref, k0_ref, k1_ref, k2_ref, kc_ref, v0_ref, v1_ref, v2_ref, vc_ref, band_ref, sink_ref, o_ref):
    n = pl.program_id(1)
    lo = _low_half((SWA_BLOCK, LANES))
    is_ctx = n >= N_QBLK_LAT
    pen0 = jnp.where((n == 0) | is_ctx, NEG, 0.0)
    pen1 = jnp.where(is_ctx, NEG, 0.0)
    pen2 = jnp.where(n >= N_QBLK_LAT - 1, NEG, 0.0)
    mask0 = band_ref[0] + pen0
    mask2 = band_ref[1] + pen2
    b = SWA_BLOCK
    gw = SWA_GROUP * HEAD_DIM
    for g in range(SWA_KV_HEADS):
        parts = []
        for pr in range(SWA_GROUP // 2):
            qp = q_ref[:, g * gw + pr * LANES:g * gw + (pr + 1) * LANES]
            zero = jnp.zeros_like(qp)
            parts.append(jnp.where(lo, qp, zero))
            parts.append(jnp.where(lo, zero, qp))
        qs = jnp.concatenate(parts, axis=0)
        ks = slice(g * LANES, (g + 1) * LANES)
        vs = slice(g * HEAD_DIM, (g + 1) * HEAD_DIM)
        k_all = jnp.concatenate([r[:, ks] for r in (k0_ref, k1_ref, k2_ref, kc_ref)], axis=0)
        vt_all = jnp.concatenate([r[vs, :] for r in (v0_ref, v1_ref, v2_ref, vc_ref)], axis=1)
        s = _dot_nt(k_all, qs)
        blocks = [s[:b] + mask0, s[b:2 * b] + pen1, s[2 * b:3 * b] + mask2, s[3 * b:]]
        ot = _attend_t(blocks, vt_all, sink=sink_ref[g])
        for pr in range(SWA_GROUP // 2):
            pair = jnp.concatenate([ot[:, (2 * pr) * b:(2 * pr + 1) * b], ot[:, (2 * pr + 1) * b:(2 * pr + 2) * b]],
                                   axis=0)
            o_ref[:, g * gw + pr * LANES:g * gw + (pr + 1) * LANES] = pair.T.astype(o_ref.dtype)


def _swa_band():
    kk = np.arange(SWA_BLOCK)[:, None]
    qi = np.arange(SWA_BLOCK)[None, :]
    before = np.where(kk >= qi, 0.0, NEG)
    after = np.where(kk <= qi, 0.0, NEG)
    band = np.stack([before, after]).astype(np.float32)
    return np.tile(band, (1, 1, SWA_GROUP))


def _swa_attention(q, kd, vt, sink_row, n_qblk):
    rows = SWA_GROUP * SWA_BLOCK
    kw = SWA_KV_HEADS * LANES
    vw = SWA_KV_HEADS * HEAD_DIM
    q_spec = pl.BlockSpec((None, SWA_BLOCK, SWA_Q_HEADS * HEAD_DIM), lambda b, n: (b, n, 0))

    def near(i):
        return lambda n: jnp.clip(n - 1 + i, 0, N_QBLK_LAT - 1)

    k_specs = [pl.BlockSpec((None, SWA_BLOCK, kw), functools.partial(lambda f, b, n: (b, f(n), 0), near(i)))
               for i in range(3)]
    k_specs.append(pl.BlockSpec((None, CTX_LEN, kw), lambda b, n: (b, SEQ // CTX_LEN, 0)))
    v_specs = [pl.BlockSpec((None, vw, SWA_BLOCK), functools.partial(lambda f, b, n: (b, 0, f(n)), near(i)))
               for i in range(3)]
    v_specs.append(pl.BlockSpec((None, vw, CTX_LEN), lambda b, n: (b, 0, SEQ // CTX_LEN)))
    band_spec = _const_spec((2, SWA_BLOCK, rows))
    sink_spec = _const_spec((SWA_KV_HEADS, 1, rows))
    return pl.pallas_call(
        _swa_kernel,
        grid=(BATCH, n_qblk),
        in_specs=[q_spec] + k_specs + v_specs + [band_spec, sink_spec],
        out_specs=q_spec,
        out_shape=jax.ShapeDtypeStruct((BATCH, n_qblk * SWA_BLOCK, SWA_Q_HEADS * HEAD_DIM), BF16),
        compiler_params=_params(("arbitrary", "arbitrary")),
        name="swa_attention",
    )(q, kd, kd, kd, kd, vt, vt, vt, vt, jnp.asarray(_swa_band()), sink_row)


def _mix_ffn_kernel(n_o, n_x, n_tiles, *refs):
    o_refs, w_refs = refs[:3 * n_o], refs[3 * n_o:4 * n_o]
    x_refs = refs[4 * n_o:4 * n_o + n_x]
    (xp_ref, xn_ref, gm_ref, sh_ref, sc_ref, gf_ref, w_up_ref, b_up_ref, cw_ref, cb_ref,
     w_dn_ref, b_dn_ref, out_ref, act_ref) = refs[4 * n_o + n_x:]
    t = pl.program_id(1)
    y = None
    for i in range(n_o):
        tile, prev, nxt = o_refs[3 * i:3 * i + 3]
        d = _dot(jnp.concatenate([prev[...], tile[...], nxt[...]], axis=0), w_refs[i][...])
        y = d if y is None else y + d
    x0 = jnp.concatenate([xp_ref[...], _stream_rows(x_refs, slice(None)), xn_ref[...]], axis=0)
    x1 = _layer_norm(DEEPNORM_ALPHA * x0 + gm_ref[...] * y)
    x = x1[NBR:NBR + TM]
    rows = TM + 2 * HALO
    h = (x1[NBR - HALO:NBR + TM + HALO] * (1.0 + sc_ref[...]) + sh_ref[...]).astype(BF16)
    prev_ok = (t != 0) & (t != NT_LAT)
    next_ok = (t != NT_LAT - 1) & (t != n_tiles - 1)
    for c in range(N_FF_CHUNKS):
        b_up = b_up_ref[c]
        cw = cw_ref[c]
        w0, w1, w2 = cw[0:1], cw[1:2], cw[2:3]
        u = _dot(h, w_up_ref[c])
        u = jnp.concatenate([jnp.where(prev_ok, u[:HALO], -b_up), u[HALO:HALO + TM],
                             jnp.where(next_ok, u[HALO + TM:], -b_up)], axis=0)
        cb = cb_ref[c] + b_up * (w0 + w1 + w2)
        v = (w0 * pltpu.roll(u, 1, 0) + w1 * u + w2 * pltpu.roll(u, rows - 1, 0) + cb)[HALO:HALO + TM]
        act_ref[:, c * FF_CHUNK:(c + 1) * FF_CHUNK] = (v[:, :FF_CHUNK] * _silu(v[:, FF_CHUNK:])).astype(BF16)
    y2 = _dot(act_ref[...], w_dn_ref[...]) + b_dn_ref[...]
    out_ref[...] = _layer_norm(DEEPNORM_ALPHA * x + gf_ref[...] * y2)


def _mix_ffn(o_list, w_list, xs, gm, sh, sc, gf, w_up, b_up, cw, cb, w_dn, b_dn, n_tiles):
    per_tile = TM // NBR

    def neighbours(width, n_rows):
        last = n_rows // NBR - 1
        return [pl.BlockSpec((None, NBR, width), lambda b, t: (b, jnp.clip(t * per_tile - 1, 0, last), 0)),
                pl.BlockSpec((None, NBR, width), lambda b, t: (b, jnp.minimum((t + 1) * per_tile, last), 0))]

    o_specs, o_args = [], []
    for o in o_list:
        o_specs += [_tok_spec(o.shape[-1])] + neighbours(o.shape[-1], n_tiles * TM)
        o_args += [o, o, o]
    x_rows = xs[0].shape[1] if len(xs) > 1 else n_tiles * TM
    consts = [w_up, b_up, cw, cb, w_dn, b_dn]
    return pl.pallas_call(
        functools.partial(_mix_ffn_kernel, len(o_list), len(xs), n_tiles),
        grid=(BATCH, n_tiles),
        in_specs=o_specs + [_const_spec(w.shape) for w in w_list] + _stream_specs(len(xs) > 1)
                 + neighbours(D_MODEL, x_rows) + [_mod_spec()] * 4 + [_const_spec(a.shape) for a in consts],
        out_specs=_tok_spec(D_MODEL),
        out_shape=jax.ShapeDtypeStruct((BATCH, n_tiles * TM, D_MODEL), F32),
        scratch_shapes=[pltpu.VMEM((TM, D_FF), BF16)],
        compiler_params=_params(("arbitrary", "arbitrary")),
        name="mix_ffn_ln",
    )(*o_args, *w_list, *xs, xs[0], xs[0], gm, sh, sc, gf, *consts)


def _rope_tables(rot_dim):
    axis_dim = rot_dim // 2
    t = jnp.arange(SEQ)
    row = (t // GRID_W).astype(F32)[:, None]
    col = (t % GRID_W).astype(F32)[:, None]
    inv_freq = ROPE_BASE ** (-jnp.arange(0, axis_dim, 2, dtype=F32) / axis_dim)
    ar, ac = row * inv_freq, col * inv_freq
    ang = jnp.concatenate([ar, ar, ac, ac], axis=-1)
    sign = np.where(np.arange(rot_dim) % axis_dim < axis_dim // 2, -1.0, 1.0).astype(np.float32)
    return jnp.cos(ang), jnp.sin(ang) * sign


def _with_ctx_rows(cos, sin):
    ones = jnp.ones((CTX_LEN, cos.shape[1]), F32)
    return jnp.concatenate([cos, ones], axis=0), jnp.concatenate([sin, jnp.zeros_like(ones)], axis=0)


def _mla_tables():
    cos, sin = _rope_tables(MLA_ROPE)
    pad_l = jnp.ones((SEQ, MLA_NOPE), F32)
    pad_r = jnp.ones((SEQ, LANES - MLA_NOPE - MLA_ROPE), F32)
    cos = jnp.concatenate([pad_l, cos, pad_r], axis=-1)
    sin = jnp.concatenate([0 * pad_l, sin, 0 * pad_r], axis=-1)
    cos, sin = _with_ctx_rows(cos, sin)
    qs = (MLA_NOPE + MLA_ROPE) ** -0.5 * LOG2E
    return cos * qs, sin * qs, cos, sin


def _swa_tables():
    cos, sin = _rope_tables(HEAD_DIM)
    cos, sin = _with_ctx_rows(jnp.tile(cos, (1, 2)), jnp.tile(sin, (1, 2)))
    qs = HEAD_DIM ** -0.5 * LOG2E
    return cos * qs, sin * qs, cos, sin


def _even_weights(w_in, w_uq, w_ukv, w_out):
    o3 = 3 * NA_WIDTH + MLA_Q_RANK + MLA_KV_RANK
    pad_l = jnp.zeros((D_MODEL, MLA_NOPE), F32)
    pad_r = jnp.zeros((D_MODEL, LANES - MLA_NOPE - MLA_ROPE), F32)
    w_in_p = jnp.concatenate([w_in[:, :o3], pad_l, w_in[:, o3:], pad_r], axis=-1).astype(BF16)
    uq = w_uq.reshape(MLA_Q_RANK, MLA_HEADS, MLA_NOPE + MLA_ROPE)
    uq = jnp.pad(uq, ((0, 0), (0, 0), (0, LANES - MLA_NOPE - MLA_ROPE))).reshape(MLA_Q_RANK, MLA_HEADS * LANES)
    ukv = w_ukv.reshape(MLA_KV_RANK, MLA_HEADS, MLA_NOPE + MLA_V)
    ukk = jnp.pad(ukv[:, :, :MLA_NOPE], ((0, 0), (0, 0), (0, LANES - MLA_NOPE))).reshape(MLA_KV_RANK, MLA_HEADS * LANES)
    ukvv = ukv[:, :, MLA_NOPE:].reshape(MLA_KV_RANK, MLA_HEADS * MLA_V)
    wo = w_out.astype(BF16)
    return w_in_p, uq.astype(BF16), ukk.astype(BF16), ukvv.astype(BF16), wo[:NA_WIDTH], wo[NA_WIDTH:]


def _odd_weights(w_in, w_out):
    nq = SWA_Q_HEADS * HEAD_DIM
    nkv = SWA_KV_HEADS * HEAD_DIM
    dup = lambda w: jnp.repeat(w.reshape(D_MODEL, SWA_KV_HEADS, 1, HEAD_DIM), 2, axis=2).reshape(D_MODEL, 2 * nkv)
    w_in_p = jnp.concatenate([w_in[:, :nq], dup(w_in[:, nq:nq + nkv]), w_in[:, nq + nkv:]], axis=-1)
    return w_in_p.astype(BF16), w_out.astype(BF16)


def _ffn_weights(w_up, b_up, conv_w, conv_b, w_down, b_down):
    def chunked(a):
        lead = a.shape[:-1]
        a = a.reshape(*lead, 2, N_FF_CHUNKS, FF_CHUNK)
        a = jnp.moveaxis(a, -2, 0)
        return a.reshape(N_FF_CHUNKS, *lead, 2 * FF_CHUNK)
    return (chunked(w_up).astype(BF16), chunked(b_up[None]), chunked(conv_w), chunked(conv_b[None]),
            w_down.astype(BF16), b_down[None])


def kernel(x, c, ctx, c_ctx, w_ada, b_ada, na_rpb, w_in_even, mla_q_norm, w_uq, mla_kv_norm, w_ukv, w_out_even,
           w_in_odd, sinks, w_out_odd, w_up, b_up, conv_w, conv_b, w_down, b_down):
    cs = jnp.concatenate([c, c_ctx[None], jnp.zeros((7, D_MODEL), F32)], axis=0)
    mod = _modulation(cs, w_ada, b_ada)
    mod_lat = mod[:, :BATCH].reshape(DEPTH, BATCH, 1, 6, D_MODEL)
    mod_ctx = jnp.broadcast_to(mod[:, BATCH].reshape(DEPTH, 1, 1, 6, D_MODEL), mod_lat.shape)
    mods = jnp.concatenate([mod_lat, mod_ctx], axis=2).reshape(DEPTH, 2 * BATCH, 6, 1, D_MODEL)

    mla_tabs = _mla_tables()
    swa_tabs = _swa_tables()
    xs = [x, ctx]
    for l in range(DEPTH):
        i = l // 2
        last = l == DEPTH - 1
        n_tiles = NT_LAT if last else NT_ALL
        sh_m, sc_m, g_m, sh_f, sc_f, g_f = (mods[l, :, k] for k in range(6))
        if l % 2 == 0:
            w_in_p, uq, ukk, ukvv, wo_a, wo_b = _even_weights(w_in_even[i], w_uq[i], w_ukv[i], w_out_even[i])
            qa, ka, vat, qm, km, vmt = _inproj_even(xs, sh_m, sc_m, w_in_p, mla_q_norm[i][None], uq,
                                                    mla_kv_norm[i][None], ukk, ukvv, mla_tabs)
            o_a = _na_attention(qa, ka, vat, _na_bias(na_rpb[i]))
            o_b = _mla_attention(qm, km, vmt)
            o_list, wo_list = [o_a, o_b], [wo_a, wo_b]
        else:
            w_in_p, wo = _odd_weights(w_in_odd[i], w_out_odd[i])
            q, kd, vt = _inproj_odd(xs[0], sh_m, sc_m, w_in_p, swa_tabs)
            sink_row = jnp.repeat(sinks[i] * LOG2E, SWA_BLOCK).reshape(SWA_KV_HEADS, 1, SWA_GROUP * SWA_BLOCK)
            n_qblk = N_QBLK_LAT if last else T_ALL // SWA_BLOCK
            o = _swa_attention(q, kd, vt, sink_row, n_qblk)
            o_list, wo_list = [o], [wo]
        fw = _ffn_weights(w_up[l], b_up[l], conv_w[l], conv_b[l], w_down[l], b_down[l])
        xs = [_mix_ffn(o_list, wo_list, xs, g_m, sh_f, sc_f, g_f, *fw, n_tiles)]
    return xs[0]
```

```python
import functools

import numpy as np
import jax
import jax.numpy as jnp
from jax import lax
from jax.experimental import pallas as pl
from jax.experimental.pallas import tpu as pltpu

D_MODEL = 1024
BATCH = 16
SEQ = 2048
CTX_LEN = 256
T_ALL = SEQ + CTX_LEN
DEPTH = 4
GRID_W = 64
HEAD_DIM = 64
NA_HEADS = 8
NA_WIN_R = 8
NA_WIN_C = 16
MLA_HEADS = 8
MLA_Q_RANK = 384
MLA_KV_RANK = 256
MLA_NOPE = 64
MLA_ROPE = 32
MLA_V = 64
SWA_Q_HEADS = 16
SWA_KV_HEADS = 2
SWA_WINDOW = 128
SWA_BLOCK = 128
D_FF = 2816
CONV_W = 3
ROPE_BASE = 10000.0
LN_EPS = 1e-6
RMS_EPS = 1e-6
NEG = -1e30
DEEPNORM_ALPHA = (2 * DEPTH) ** 0.25
NA_WIDTH = NA_HEADS * HEAD_DIM

LANES = 128
TM = 256
NT_LAT = SEQ // TM
NT_ALL = T_ALL // TM
HALO = 8
NBR = 16
ROW_GROUP = 128
SUM_ROWS = 16
FF_CHUNK = 256
N_FF_CHUNKS = D_FF // FF_CHUNK
NA_QROWS = TM // GRID_W
NA_KROWS = 3 * NA_QROWS
VMEM_LIMIT = 56 * 1024 * 1024
LOG2E = 1.4426950408889634
PAIRS_PER_STEP = 4

F32 = jnp.float32
BF16 = jnp.bfloat16


def _params(sem):
    return pltpu.CompilerParams(dimension_semantics=sem, vmem_limit_bytes=VMEM_LIMIT)


def _dot(a, b):
    return jnp.dot(a, b, preferred_element_type=F32)


def _dot_nt(a, b):
    return lax.dot_general(a, b, (((1,), (1,)), ((), ())), preferred_element_type=F32)


def _layer_norm(x):
    mu = jnp.mean(x, axis=-1, keepdims=True)
    xc = x - mu
    var = jnp.mean(xc * xc, axis=-1, keepdims=True)
    return xc * lax.rsqrt(var + LN_EPS)


def _rms_norm(x, g):
    return x * lax.rsqrt(jnp.mean(x * x, axis=-1, keepdims=True) + RMS_EPS) * g


def _silu(x):
    return x / (1.0 + jnp.exp(-x))


def _rope(x, cos, sin, chunk):
    lane = lax.broadcasted_iota(jnp.int32, x.shape, 1)
    first = (lane % (2 * chunk)) < chunk
    rot = jnp.where(first, pltpu.roll(x, LANES - chunk, 1), pltpu.roll(x, chunk, 1))
    return x * cos + rot * sin


def _attend_t(s_list, vt, sink=None, vpu_bound=True):
    dv, n_keys = vt.shape
    if vpu_bound:
        vt = jnp.concatenate([vt, jnp.ones((SUM_ROWS, n_keys), BF16)], axis=0)
    m = l = o = None
    k0 = 0
    for s in s_list:
        k1 = k0 + s.shape[0]
        m_blk = jnp.max(s, axis=0, keepdims=True)
        m_new = m_blk if m is None else jnp.maximum(m, m_blk)
        if vpu_bound:
            pv_sum = _dot(vt[:, k0:k1], jnp.exp2((s - m_new).astype(BF16)))
            pv, p_sum = pv_sum[:dv], pv_sum[dv:dv + 1]
        else:
            p = jnp.exp2(s - m_new)
            p_sum = jnp.sum(p, axis=0, keepdims=True)
            pv = _dot(vt[:, k0:k1], p.astype(BF16))
        if m is None:
            l, o = p_sum, pv
        else:
            alpha = jnp.exp2(m - m_new)
            l = l * alpha + p_sum
            o = o * alpha + pv
        m, k0 = m_new, k1
    if sink is not None:
        m_new = jnp.maximum(m, sink)
        alpha = jnp.exp2(m - m_new)
        l = l * alpha + jnp.exp2(sink - m_new)
        o = o * alpha
    return o * (1.0 / l)


def _low_half(shape):
    return lax.broadcasted_iota(jnp.int32, shape, 1) < HEAD_DIM


def _pack_heads_t(ot0, ot1):
    top = lax.broadcasted_iota(jnp.int32, ot0.shape, 0) < HEAD_DIM
    return jnp.where(top, ot0, ot1).T


def _mod_kernel(c_ref, w_ref, b_ref, o_ref):
    a = _silu(c_ref[...]).astype(BF16)
    o_ref[...] = _dot(a, w_ref[...].astype(BF16)) + b_ref[...]


def _modulation(cs, w_ada, b_ada):
    rows = cs.shape[0]
    tn = 1536
    return pl.pallas_call(
        _mod_kernel,
        grid=(DEPTH, 6 * D_MODEL // tn),
        in_specs=[
            pl.BlockSpec((rows, D_MODEL), lambda l, n: (0, 0)),
            pl.BlockSpec((None, D_MODEL, tn), lambda l, n: (l, 0, n)),
            pl.BlockSpec((None, 1, tn), lambda l, n: (l, 0, n)),
        ],
        out_specs=pl.BlockSpec((None, rows, tn), lambda l, n: (l, 0, n)),
        out_shape=jax.ShapeDtypeStruct((DEPTH, rows, 6 * D_MODEL), F32),
        compiler_params=_params(("arbitrary", "arbitrary")),
        name="adaln_mod",
    )(cs, w_ada, b_ada.reshape(DEPTH, 1, 6 * D_MODEL))


def _mod_spec():
    return pl.BlockSpec((None, 1, D_MODEL), lambda b, t: (2 * b + jnp.where(t >= NT_LAT, 1, 0), 0, 0))


def _const_spec(shape):
    return pl.BlockSpec(shape, lambda *_: (0,) * len(shape))


def _tok_spec(width):
    return pl.BlockSpec((None, TM, width), lambda b, t: (b, t, 0))


def _tok_spec_t(width):
    return pl.BlockSpec((None, width, TM), lambda b, t: (b, 0, t))


def _stream_specs(split):
    if not split:
        return [_tok_spec(D_MODEL)]
    return [pl.BlockSpec((None, TM, D_MODEL), lambda b, t: (b, jnp.minimum(t, NT_LAT - 1), 0)),
            pl.BlockSpec((None, CTX_LEN, D_MODEL), lambda b, t: (b, 0, 0))]


def _stream_rows(x_refs, rs):
    if len(x_refs) == 1:
        return x_refs[0][rs, :]
    return jnp.where(pl.program_id(1) >= NT_LAT, x_refs[1][rs, :], x_refs[0][rs, :])


def _inproj_even_kernel(n_x, *refs):
    x_refs = refs[:n_x]
    (sh_ref, sc_ref, w_in_ref, qg_ref, w_uq_ref, kg_ref, w_ukk_ref, w_ukv_ref, cq_ref, sq_ref, ck_ref, sk_ref,
     qa_ref, ka_ref, va_ref, qm_ref, km_ref, vm_ref) = refs[n_x:]
    w = NA_WIDTH
    o1 = 3 * w
    o2 = o1 + MLA_Q_RANK
    o3 = o2 + MLA_KV_RANK
    scale = 1.0 + sc_ref[...]
    shift = sh_ref[...]
    for rg in range(TM // ROW_GROUP):
        rs = slice(rg * ROW_GROUP, (rg + 1) * ROW_GROUP)
        h = (_stream_rows(x_refs, rs) * scale + shift).astype(BF16)
        z = _dot(h, w_in_ref[...])
        qa_ref[rs, :] = (z[:, :w] * (HEAD_DIM ** -0.5 * LOG2E)).astype(BF16)
        ka_ref[rs, :] = z[:, w:2 * w].astype(BF16)
        va_ref[:, rs] = z[:, 2 * w:3 * w].T.astype(BF16)
        cq = _rms_norm(z[:, o1:o2], qg_ref[...]).astype(BF16)
        ckv = _rms_norm(z[:, o2:o3], kg_ref[...]).astype(BF16)
        q = _dot(cq, w_uq_ref[...])
        kn = _dot(ckv, w_ukk_ref[...])
        vm_ref[:, rs] = _dot(ckv, w_ukv_ref[...]).T.astype(BF16)
        kr = _rope(z[:, o3:o3 + LANES], ck_ref[rs, :], sk_ref[rs, :], MLA_ROPE // 4)
        cq_t = cq_ref[rs, :]
        sq_t = sq_ref[rs, :]
        for hh in range(MLA_HEADS):
            sl = slice(hh * LANES, (hh + 1) * LANES)
            qm_ref[rs, sl] = _rope(q[:, sl], cq_t, sq_t, MLA_ROPE // 4).astype(BF16)
            km_ref[rs, sl] = (kn[:, sl] + kr).astype(BF16)


def _inproj_even(xs, sh, sc, w_in, qg, w_uq, kg, w_ukk, w_ukv, tabs):
    cq, sq, ck, sk = tabs
    tab = pl.BlockSpec((TM, LANES), lambda b, t: (t, 0))
    n_in = w_in.shape[1]
    outs = [(NA_WIDTH, False), (NA_WIDTH, False), (NA_WIDTH, True),
            (MLA_HEADS * LANES, False), (MLA_HEADS * LANES, False), (MLA_HEADS * MLA_V, True)]
    return pl.pallas_call(
        functools.partial(_inproj_even_kernel, len(xs)),
        grid=(BATCH, NT_ALL),
        in_specs=_stream_specs(len(xs) > 1) + [_mod_spec(), _mod_spec(),
                  _const_spec((D_MODEL, n_in)),
                  _const_spec((1, MLA_Q_RANK)), _const_spec(w_uq.shape),
                  _const_spec((1, MLA_KV_RANK)), _const_spec(w_ukk.shape), _const_spec(w_ukv.shape),
                  tab, tab, tab, tab],
        out_specs=[_tok_spec_t(wd) if tr else _tok_spec(wd) for wd, tr in outs],
        out_shape=[jax.ShapeDtypeStruct((BATCH, wd, T_ALL) if tr else (BATCH, T_ALL, wd), BF16) for wd, tr in outs],
        compiler_params=_params(("arbitrary", "arbitrary")),
        name="inproj_even",
    )(*xs, sh, sc, w_in, qg, w_uq, kg, w_ukk, w_ukv, cq, sq, ck, sk)


def _na_kernel(q_ref, k0_ref, k1_ref, k2_ref, kc_ref, v0_ref, v1_ref, v2_ref, vc_ref, bias_ref, o_ref):
    j = pl.program_id(0)
    lo = _low_half((TM, LANES))
    n_loc = 3 * TM

    def stacked_queries(pp):
        q = q_ref[:, pp * LANES:(pp + 1) * LANES]
        zero = jnp.zeros_like(q)
        return jnp.concatenate([jnp.where(lo, q, zero), jnp.where(lo, zero, q)], axis=0)

    def store(pp, ot):
        o_ref[:, pp * LANES:(pp + 1) * LANES] = _pack_heads_t(ot[:, :TM], ot[:, TM:]).astype(o_ref.dtype)

    @pl.when(j < NT_LAT)
    def _():
        for pp in range(PAIRS_PER_STEP):
            cs = slice(pp * LANES, (pp + 1) * LANES)
            k_all = jnp.concatenate([r[:, cs] for r in (k0_ref, k1_ref, k2_ref, kc_ref)], axis=0)
            vt_all = jnp.concatenate([r[cs, :] for r in (v0_ref, v1_ref, v2_ref, vc_ref)], axis=1)
            s = _dot_nt(k_all, stacked_queries(pp))
            blocks = [s[i * TM:(i + 1) * TM] + bias_ref[pp, i * TM:(i + 1) * TM, :] for i in range(3)]
            store(pp, _attend_t(blocks + [s[n_loc:]], vt_all))

    @pl.when(j >= NT_LAT)
    def _():
        for pp in range(PAIRS_PER_STEP):
            cs = slice(pp * LANES, (pp + 1) * LANES)
            store(pp, _attend_t([_dot_nt(kc_ref[:, cs], stacked_queries(pp))], vc_ref[cs, :]))


def _na_attention(qa, ka, vat, bias):
    def kstart(j):
        return jnp.clip(j - 1, 0, NT_LAT - 3)

    def bias_class(j):
        return jnp.where(j == 0, 0, jnp.where(j == NT_LAT - 1, 2, 1))

    pps = PAIRS_PER_STEP
    blk = (None, TM, pps * LANES)
    blk_t = (None, pps * LANES, TM)
    q_spec = pl.BlockSpec(blk, lambda j, hp, b: (b, j, hp))
    k_specs = [pl.BlockSpec(blk, functools.partial(lambda i, j, hp, b: (b, kstart(j) + i, hp), i))
               for i in range(3)]
    k_specs.append(pl.BlockSpec(blk, lambda j, hp, b: (b, NT_LAT, hp)))
    v_specs = [pl.BlockSpec(blk_t, functools.partial(lambda i, j, hp, b: (b, hp, kstart(j) + i), i))
               for i in range(3)]
    v_specs.append(pl.BlockSpec(blk_t, lambda j, hp, b: (b, hp, NT_LAT)))
    bias_spec = pl.BlockSpec((None, pps, 3 * TM, 2 * TM), lambda j, hp, b: (bias_class(j), hp, 0, 0))
    return pl.pallas_call(
        _na_kernel,
        grid=(NT_ALL, NA_HEADS // (2 * pps), BATCH),
        in_specs=[q_spec] + k_specs + v_specs + [bias_spec],
        out_specs=q_spec,
        out_shape=jax.ShapeDtypeStruct((BATCH, T_ALL, NA_WIDTH), BF16),
        compiler_params=_params(("arbitrary", "arbitrary", "arbitrary")),
        name="na_attention",
    )(qa, ka, ka, ka, ka, vat, vat, vat, vat, bias)


def _na_bias_tables():
    rows = SEQ // GRID_W
    n_dr, n_dc = 2 * NA_WIN_R - 1, 2 * NA_WIN_C - 1
    kc = np.arange(GRID_W)
    qc = np.arange(GRID_W)
    w_start = np.clip(qc - NA_WIN_C // 2, 0, GRID_W - NA_WIN_C)
    col_ok = (kc[None, :] >= w_start[:, None]) & (kc[None, :] < w_start[:, None] + NA_WIN_C)
    dcol = np.clip(kc[None, :] - qc[:, None] + NA_WIN_C - 1, 0, n_dc - 1)
    col_sel = (dcol[None] == np.arange(n_dc)[:, None, None]).astype(np.float32)
    row_ok, row_sel = [], []
    for j in (0, 1, NT_LAT - 1):
        r = NA_QROWS * j + np.arange(NA_QROWS)
        r0 = np.clip(r - NA_WIN_R // 2, 0, rows - NA_WIN_R)
        start = NA_QROWS * int(np.clip(j - 1, 0, NT_LAT - 3))
        krow = start + np.arange(NA_KROWS)
        row_ok.append((krow[None, :] >= r0[:, None]) & (krow[None, :] < r0[:, None] + NA_WIN_R))
        drow = np.clip(krow[None, :] - r[:, None] + NA_WIN_R - 1, 0, n_dr - 1)
        row_sel.append((drow[..., None] == np.arange(n_dr)).astype(np.float32))
    return col_sel, col_ok, np.stack(row_sel), np.stack(row_ok)


def _na_bias(rpb):
    col_sel, col_ok, row_sel, row_ok = _na_bias_tables()
    hi = lax.Precision.HIGHEST
    by_col = jnp.einsum('hrd,dck->hrkc', rpb, col_sel, precision=hi)
    by_col = by_col.reshape(NA_HEADS // 2, 2, *by_col.shape[1:])
    full = jnp.einsum('samr,perkc->spmkeac', row_sel, by_col, precision=hi)
    ok = (row_ok.transpose(0, 2, 1)[:, None, :, None, None, :, None]
          & col_ok.T[None, None, None, :, None, None, :])
    full = jnp.where(ok, full * LOG2E, NEG)
    return full.reshape(3, NA_HEADS // 2, NA_KROWS * GRID_W, 2 * NA_QROWS * GRID_W)


MLA_KCHUNK = 256


def _mla_kernel(q_ref, k_ref, vt_ref, o_ref):
    jq = pl.program_id(2)
    first = lax.broadcasted_iota(jnp.int32, (TM, 2 * LANES), 1) < LANES

    def run(k_lo, k_hi):
        for pp in range(PAIRS_PER_STEP):
            cs = slice(pp * 2 * LANES, (pp + 1) * 2 * LANES)
            q = q_ref[:, cs]
            zero = jnp.zeros_like(q)
            qs = jnp.concatenate([jnp.where(first, q, zero), jnp.where(first, zero, q)], axis=0)
            s = _dot_nt(k_ref[k_lo:k_hi, cs], qs)
            n = k_hi - k_lo
            chunks = [s[c0:min(c0 + MLA_KCHUNK, n)] for c0 in range(0, n, MLA_KCHUNK)]
            ot = _attend_t(chunks, vt_ref[pp * LANES:(pp + 1) * LANES, k_lo:k_hi], vpu_bound=False)
            o_ref[:, pp * LANES:(pp + 1) * LANES] = _pack_heads_t(ot[:, :TM], ot[:, TM:]).astype(o_ref.dtype)

    pl.when(jq < NT_LAT)(lambda: run(0, T_ALL))
    pl.when(jq >= NT_LAT)(lambda: run(SEQ, T_ALL))


def _mla_attention(qm, km, vmt):
    pps = PAIRS_PER_STEP
    return pl.pallas_call(
        _mla_kernel,
        grid=(BATCH, MLA_HEADS // (2 * pps), NT_ALL),
        in_specs=[pl.BlockSpec((None, TM, pps * 2 * LANES), lambda b, hp, j: (b, j, hp)),
                  pl.BlockSpec((None, T_ALL, pps * 2 * LANES), lambda b, hp, j: (b, 0, hp)),
                  pl.BlockSpec((None, pps * LANES, T_ALL), lambda b, hp, j: (b, hp, 0))],
        out_specs=pl.BlockSpec((None, TM, pps * LANES), lambda b, hp, j: (b, j, hp)),
        out_shape=jax.ShapeDtypeStruct((BATCH, T_ALL, MLA_HEADS * MLA_V), BF16),
        compiler_params=_params(("arbitrary", "arbitrary", "arbitrary")),
        name="mla_attention",
    )(qm, km, vmt)


def _inproj_odd_kernel(x_ref, sh_ref, sc_ref, w_in_ref, cq_ref, sq_ref, ck_ref, sk_ref, q_ref, k_ref, v_ref):
    nq = SWA_Q_HEADS * HEAD_DIM
    nk = 2 * SWA_KV_HEADS * HEAD_DIM
    scale = 1.0 + sc_ref[...]
    shift = sh_ref[...]
    for rg in range(TM // ROW_GROUP):
        rs = slice(rg * ROW_GROUP, (rg + 1) * ROW_GROUP)
        h = (x_ref[rs, :] * scale + shift).astype(BF16)
        z = _dot(h, w_in_ref[...])
        cq_t, sq_t, ck_t, sk_t = cq_ref[rs, :], sq_ref[rs, :], ck_ref[rs, :], sk_ref[rs, :]
        for g in range(nq // LANES):
            sl = slice(g * LANES, (g + 1) * LANES)
            q_ref[rs, sl] = _rope(z[:, sl], cq_t, sq_t, HEAD_DIM // 4).astype(BF16)
        for g in range(nk // LANES):
            sl = slice(g * LANES, (g + 1) * LANES)
            k_ref[rs, sl] = _rope(z[:, nq + g * LANES:nq + (g + 1) * LANES], ck_t, sk_t,
                                  HEAD_DIM // 4).astype(BF16)
        v_ref[:, rs] = z[:, nq + nk:].T.astype(BF16)


def _inproj_odd(x, sh, sc, w_in, tabs):
    cq, sq, ck, sk = tabs
    tab = pl.BlockSpec((TM, LANES), lambda b, t: (t, 0))
    nq = SWA_Q_HEADS * HEAD_DIM
    nk = 2 * SWA_KV_HEADS * HEAD_DIM
    nv = SWA_KV_HEADS * HEAD_DIM
    return pl.pallas_call(
        _inproj_odd_kernel,
        grid=(BATCH, NT_ALL),
        in_specs=[_tok_spec(D_MODEL), _mod_spec(), _mod_spec(), _const_spec(w_in.shape), tab, tab, tab, tab],
        out_specs=[_tok_spec(nq), _tok_spec(nk), _tok_spec_t(nv)],
        out_shape=[jax.ShapeDtypeStruct((BATCH, T_ALL, nq), BF16), jax.ShapeDtypeStruct((BATCH, T_ALL, nk), BF16),
                   jax.ShapeDtypeStruct((BATCH, nv, T_ALL), BF16)],
        compiler_params=_params(("arbitrary", "arbitrary")),
        name="inproj_odd",
    )(x, sh, sc, w_in, cq, sq, ck, sk)


SWA_GROUP = SWA_Q_HEADS // SWA_KV_HEADS
N_QBLK_LAT = SEQ // SWA_BLOCK


def _swa_kernel(q_ref, k0_ref, k1_ref, k2_ref, kc_ref, v0_ref, v1_ref, v2_ref, vc_ref, band_ref, sink_ref, o_ref):
    n = pl.program_id(1)
    lo = _low_half((SWA_BLOCK, LANES))
    is_ctx = n >= N_QBLK_LAT
    pen0 = jnp.where((n == 0) | is_ctx, NEG, 0.0)
    pen1 = jnp.where(is_ctx, NEG, 0.0)
    pen2 = jnp.where(n >= N_QBLK_LAT - 1, NEG, 0.0)
    mask0 = band_ref[0] + pen0
    mask2 = band_ref[1] + pen2
    b = SWA_BLOCK
    gw = SWA_GROUP * HEAD_DIM
    for g in range(SWA_KV_HEADS):
        parts = []
        for pr in range(SWA_GROUP // 2):
            qp = q_ref[:, g * gw + pr * LANES:g * gw + (pr + 1) * LANES]
            zero = jnp.zeros_like(qp)
            parts.append(jnp.where(lo, qp, zero))
            parts.append(jnp.where(lo, zero, qp))
        qs = jnp.concatenate(parts, axis=0)
        ks = slice(g * LANES, (g + 1) * LANES)
        vs = slice(g * HEAD_DIM, (g + 1) * HEAD_DIM)
        k_all = jnp.concatenate([r[:, ks] for r in (k0_ref, k1_ref, k2_ref, kc_ref)], axis=0)
        vt_all = jnp.concatenate([r[vs, :] for r in (v0_ref, v1_ref, v2_ref, vc_ref)], axis=1)
        s = _dot_nt(k_all, qs)
        blocks = [s[:b] + mask0, s[b:2 * b] + pen1, s[2 * b:3 * b] + mask2, s[3 * b:]]
        ot = _attend_t(blocks, vt_all, sink=sink_ref[g])
        for pr in range(SWA_GROUP // 2):
            pair = jnp.concatenate([ot[:, (2 * pr) * b:(2 * pr + 1) * b], ot[:, (2 * pr + 1) * b:(2 * pr + 2) * b]],
                                   axis=0)
            o_ref[:, g * gw + pr * LANES:g * gw + (pr + 1) * LANES] = pair.T.astype(o_ref.dtype)


def _swa_band():
    kk = np.arange(SWA_BLOCK)[:, None]
    qi = np.arange(SWA_BLOCK)[None, :]
    before = np.where(kk >= qi, 0.0, NEG)
    after = np.where(kk <= qi, 0.0, NEG)
    band = np.stack([before, after]).astype(np.float32)
    return np.tile(band, (1, 1, SWA_GROUP))


def _swa_attention(q, kd, vt, sink_row, n_qblk):
    rows = SWA_GROUP * SWA_BLOCK
    kw = SWA_KV_HEADS * LANES
    vw = SWA_KV_HEADS * HEAD_DIM
    q_spec = pl.BlockSpec((None, SWA_BLOCK, SWA_Q_HEADS * HEAD_DIM), lambda b, n: (b, n, 0))

    def near(i):
        return lambda n: jnp.clip(n - 1 + i, 0, N_QBLK_LAT - 1)

    k_specs = [pl.BlockSpec((None, SWA_BLOCK, kw), functools.partial(lambda f, b, n: (b, f(n), 0), near(i)))
               for i in range(3)]
    k_specs.append(pl.BlockSpec((None, CTX_LEN, kw), lambda b, n: (b, SEQ // CTX_LEN, 0)))
    v_specs = [pl.BlockSpec((None, vw, SWA_BLOCK), functools.partial(lambda f, b, n: (b, 0, f(n)), near(i)))
               for i in range(3)]
    v_specs.append(pl.BlockSpec((None, vw, CTX_LEN), lambda b, n: (b, 0, SEQ // CTX_LEN)))
    band_spec = _const_spec((2, SWA_BLOCK, rows))
    sink_spec = _const_spec((SWA_KV_HEADS, 1, rows))
    return pl.pallas_call(
        _swa_kernel,
        grid=(BATCH, n_qblk),
        in_specs=[q_spec] + k_specs + v_specs + [band_spec, sink_spec],
        out_specs=q_spec,
        out_shape=jax.ShapeDtypeStruct((BATCH, n_qblk * SWA_BLOCK, SWA_Q_HEADS * HEAD_DIM), BF16),
        compiler_params=_params(("arbitrary", "arbitrary")),
        name="swa_attention",
    )(q, kd, kd, kd, kd, vt, vt, vt, vt, jnp.asarray(_swa_band()), sink_row)


def _mix_ffn_kernel(n_o, n_x, n_tiles, *refs):
    o_refs, w_refs = refs[:3 * n_o], refs[3 * n_o:4 * n_o]
    x_refs = refs[4 * n_o:4 * n_o + n_x]
    (xp_ref, xn_ref, gm_ref, sh_ref, sc_ref, gf_ref, w_up_ref, b_up_ref, cw_ref, cb_ref,
     w_dn_ref, b_dn_ref, out_ref, act_ref) = refs[4 * n_o + n_x:]
    t = pl.program_id(1)
    y = None
    for i in range(n_o):
        tile, prev, nxt = o_refs[3 * i:3 * i + 3]
        d = _dot(jnp.concatenate([prev[...], tile[...], nxt[...]], axis=0), w_refs[i][...])
        y = d if y is None else y + d
    x0 = jnp.concatenate([xp_ref[...], _stream_rows(x_refs, slice(None)), xn_ref[...]], axis=0)
    x1 = _layer_norm(DEEPNORM_ALPHA * x0 + gm_ref[...] * y)
    x = x1[NBR:NBR + TM]
    rows = TM + 2 * HALO
    h = (x1[NBR - HALO:NBR + TM + HALO] * (1.0 + sc_ref[...]) + sh_ref[...]).astype(BF16)
    prev_ok = (t != 0) & (t != NT_LAT)
    next_ok = (t != NT_LAT - 1) & (t != n_tiles - 1)
    for c in range(N_FF_CHUNKS):
        b_up = b_up_ref[c]
        cw = cw_ref[c]
        w0, w1, w2 = cw[0:1], cw[1:2], cw[2:3]
        u = _dot(h, w_up_ref[c])
        u = jnp.concatenate([jnp.where(prev_ok, u[:HALO], -b_up), u[HALO:HALO + TM],
                             jnp.where(next_ok, u[HALO + TM:], -b_up)], axis=0)
        cb = cb_ref[c] + b_up * (w0 + w1 + w2)
        v = (w0 * pltpu.roll(u, 1, 0) + w1 * u + w2 * pltpu.roll(u, rows - 1, 0) + cb)[HALO:HALO + TM]
        act_ref[:, c * FF_CHUNK:(c + 1) * FF_CHUNK] = (v[:, :FF_CHUNK] * _silu(v[:, FF_CHUNK:])).astype(BF16)
    y2 = _dot(act_ref[...], w_dn_ref[...]) + b_dn_ref[...]
    out_ref[...] = _layer_norm(DEEPNORM_ALPHA * x + gf_ref[...] * y2)


def _mix_ffn(o_list, w_list, xs, gm, sh, sc, gf, w_up, b_up, cw, cb, w_dn, b_dn, n_tiles):
    per_tile = TM // NBR

    def neighbours(width, n_rows):
        last = n_rows // NBR - 1
        return [pl.BlockSpec((None, NBR, width), lambda b, t: (b, jnp.clip(t * per_tile - 1, 0, last), 0)),
                pl.BlockSpec((None, NBR, width), lambda b, t: (b, jnp.minimum((t + 1) * per_tile, last), 0))]

    o_specs, o_args = [], []
    for o in o_list:
        o_specs += [_tok_spec(o.shape[-1])] + neighbours(o.shape[-1], n_tiles * TM)
        o_args += [o, o, o]
    x_rows = xs[0].shape[1] if len(xs) > 1 else n_tiles * TM
    consts = [w_up, b_up, cw, cb, w_dn, b_dn]
    return pl.pallas_call(
        functools.partial(_mix_ffn_kernel, len(o_list), len(xs), n_tiles),
        grid=(BATCH, n_tiles),
        in_specs=o_specs + [_const_spec(w.shape) for w in w_list] + _stream_specs(len(xs) > 1)
                 + neighbours(D_MODEL, x_rows) + [_mod_spec()] * 4 + [_const_spec(a.shape) for a in consts],
        out_specs=_tok_spec(D_MODEL),
        out_shape=jax.ShapeDtypeStruct((BATCH, n_tiles * TM, D_MODEL), F32),
        scratch_shapes=[pltpu.VMEM((TM, D_FF), BF16)],
        compiler_params=_params(("arbitrary", "arbitrary")),
        name="mix_ffn_ln",
    )(*o_args, *w_list, *xs, xs[0], xs[0], gm, sh, sc, gf, *consts)


def _rope_tables(rot_dim):
    axis_dim = rot_dim // 2
    t = jnp.arange(SEQ)
    row = (t // GRID_W).astype(F32)[:, None]
    col = (t % GRID_W).astype(F32)[:, None]
    inv_freq = ROPE_BASE ** (-jnp.arange(0, axis_dim, 2, dtype=F32) / axis_dim)
    ar, ac = row * inv_freq, col * inv_freq
    ang = jnp.concatenate([ar, ar, ac, ac], axis=-1)
    sign = np.where(np.arange(rot_dim) % axis_dim < axis_dim // 2, -1.0, 1.0).astype(np.float32)
    return jnp.cos(ang), jnp.sin(ang) * sign


def _with_ctx_rows(cos, sin):
    ones = jnp.ones((CTX_LEN, cos.shape[1]), F32)
    return jnp.concatenate([cos, ones], axis=0), jnp.concatenate([sin, jnp.zeros_like(ones)], axis=0)


def _mla_tables():
    cos, sin = _rope_tables(MLA_ROPE)
    pad_l = jnp.ones((SEQ, MLA_NOPE), F32)
    pad_r = jnp.ones((SEQ, LANES - MLA_NOPE - MLA_ROPE), F32)
    cos = jnp.concatenate([pad_l, cos, pad_r], axis=-1)
    sin = jnp.concatenate([0 * pad_l, sin, 0 * pad_r], axis=-1)
    cos, sin = _with_ctx_rows(cos, sin)
    qs = (MLA_NOPE + MLA_ROPE) ** -0.5 * LOG2E
    return cos * qs, sin * qs, cos, sin


def _swa_tables():
    cos, sin = _rope_tables(HEAD_DIM)
    cos, sin = _with_ctx_rows(jnp.tile(cos, (1, 2)), jnp.tile(sin, (1, 2)))
    qs = HEAD_DIM ** -0.5 * LOG2E
    return cos * qs, sin * qs, cos, sin


def _even_weights(w_in, w_uq, w_ukv, w_out):
    o3 = 3 * NA_WIDTH + MLA_Q_RANK + MLA_KV_RANK
    pad_l = jnp.zeros((D_MODEL, MLA_NOPE), F32)
    pad_r = jnp.zeros((D_MODEL, LANES - MLA_NOPE - MLA_ROPE), F32)
    w_in_p = jnp.concatenate([w_in[:, :o3], pad_l, w_in[:, o3:], pad_r], axis=-1).astype(BF16)
    uq = w_uq.reshape(MLA_Q_RANK, MLA_HEADS, MLA_NOPE + MLA_ROPE)
    uq = jnp.pad(uq, ((0, 0), (0, 0), (0, LANES - MLA_NOPE - MLA_ROPE))).reshape(MLA_Q_RANK, MLA_HEADS * LANES)
    ukv = w_ukv.reshape(MLA_KV_RANK, MLA_HEADS, MLA_NOPE + MLA_V)
    ukk = jnp.pad(ukv[:, :, :MLA_NOPE], ((0, 0), (0, 0), (0, LANES - MLA_NOPE))).reshape(MLA_KV_RANK, MLA_HEADS * LANES)
    ukvv = ukv[:, :, MLA_NOPE:].reshape(MLA_KV_RANK, MLA_HEADS * MLA_V)
    wo = w_out.astype(BF16)
    return w_in_p, uq.astype(BF16), ukk.astype(BF16), ukvv.astype(BF16), wo[:NA_WIDTH], wo[NA_WIDTH:]


def _odd_weights(w_in, w_out):
    nq = SWA_Q_HEADS * HEAD_DIM
    nkv = SWA_KV_HEADS * HEAD_DIM
    dup = lambda w: jnp.repeat(w.reshape(D_MODEL, SWA_KV_HEADS, 1, HEAD_DIM), 2, axis=2).reshape(D_MODEL, 2 * nkv)
    w_in_p = jnp.concatenate([w_in[:, :nq], dup(w_in[:, nq:nq + nkv]), w_in[:, nq + nkv:]], axis=-1)
    return w_in_p.astype(BF16), w_out.astype(BF16)


def _ffn_weights(w_up, b_up, conv_w, conv_b, w_down, b_down):
    def chunked(a):
        lead = a.shape[:-1]
        a = a.reshape(*lead, 2, N_FF_CHUNKS, FF_CHUNK)
        a = jnp.moveaxis(a, -2, 0)
        return a.reshape(N_FF_CHUNKS, *lead, 2 * FF_CHUNK)
    return (chunked(w_up).astype(BF16), chunked(b_up[None]), chunked(conv_w), chunked(conv_b[None]),
            w_down.astype(BF16), b_down[None])


def kernel(x, c, ctx, c_ctx, w_ada, b_ada, na_rpb, w_in_even, mla_q_norm, w_uq, mla_kv_norm, w_ukv, w_out_even,
           w_in_odd, sinks, w_out_odd, w_up, b_up, conv_w, conv_b, w_down, b_down):
    cs = jnp.concatenate([c, c_ctx[None], jnp.zeros((7, D_MODEL), F32)], axis=0)
    mod = _modulation(cs, w_ada, b_ada)
    mod_lat = mod[:, :BATCH].reshape(DEPTH, BATCH, 1, 6, D_MODEL)
    mod_ctx = jnp.broadcast_to(mod[:, BATCH].reshape(DEPTH, 1, 1, 6, D_MODEL), mod_lat.shape)
    mods = jnp.concatenate([mod_lat, mod_ctx], axis=2).reshape(DEPTH, 2 * BATCH, 6, 1, D_MODEL)

    mla_tabs = _mla_tables()
    swa_tabs = _swa_tables()
    xs = [x, ctx]
    for l in range(DEPTH):
        i = l // 2
        last = l == DEPTH - 1
        n_tiles = NT_LAT if last else NT_ALL
        sh_m, sc_m, g_m, sh_f, sc_f, g_f = (mods[l, :, k] for k in range(6))
        if l % 2 == 0:
            w_in_p, uq, ukk, ukvv, wo_a, wo_b = _even_weights(w_in_even[i], w_uq[i], w_ukv[i], w_out_even[i])
            qa, ka, vat, qm, km, vmt = _inproj_even(xs, sh_m, sc_m, w_in_p, mla_q_norm[i][None], uq,
                                                    mla_kv_norm[i][None], ukk, ukvv, mla_tabs)
            o_a = _na_attention(qa, ka, vat, _na_bias(na_rpb[i]))
            o_b = _mla_attention(qm, km, vmt)
            o_list, wo_list = [o_a, o_b], [wo_a, wo_b]
        else:
            w_in_p, wo = _odd_weights(w_in_odd[i], w_out_odd[i])
            q, kd, vt = _inproj_odd(xs[0], sh_m, sc_m, w_in_p, swa_tabs)
            sink_row = jnp.repeat(sinks[i] * LOG2E, SWA_BLOCK).reshape(SWA_KV_HEADS, 1, SWA_GROUP * SWA_BLOCK)
            n_qblk = N_QBLK_LAT if last else T_ALL // SWA_BLOCK
            o = _swa_attention(q, kd, vt, sink_row, n_qblk)
            o_list, wo_list = [o], [wo]
        fw = _ffn_weights(w_up[l], b_up[l], conv_w[l], conv_b[l], w_down[l], b_down[l])
        xs = [_mix_ffn(o_list, wo_list, xs, g_m, sh_f, sc_f, g_f, *fw, n_tiles)]
    return xs[0]
```
